```python
import math
import numpy as np
import jax
import jax.numpy as jnp
from jax import lax

D_MODEL = 1024
BATCH = 32
SEQ = 2048
DEPTH = 2

GRID_W = 64
CTX_LEN = 256
N_MOD = 6
RMS_EPS = 1e-6
ROPE_THETA = 10000.0
BRANCH_WIDTH = D_MODEL // 2
N_BRANCHES = 4
HEAD_DIM = 64
GQA_Q_HEADS = BRANCH_WIDTH // HEAD_DIM
GQA_KV_HEADS = GQA_Q_HEADS // 4
GQA_GROUP = GQA_Q_HEADS // GQA_KV_HEADS
Q_BLOCK = 128
DIFF_HEAD_DIM = 64
DIFF_HEADS = BRANCH_WIDTH // (2 * DIFF_HEAD_DIM)
RET_QK_DIM = 64
RET_V_DIM = 2 * RET_QK_DIM
RET_HEADS = BRANCH_WIDTH // RET_V_DIM
RET_CHUNK = 128
DN_HEAD_DIM = 128
DN_HEADS = BRANCH_WIDTH // DN_HEAD_DIM
DN_CONV = 5
DN_CHUNK = 64
N_EXPERTS = 32
TOP_K = 4
EXPERT_FF = D_MODEL
SWIGLU_LIMIT = 7.0
SWIGLU_ALPHA = 1.702
MOE_BLOCK = 256
IN_SIZES = (
    GQA_Q_HEADS * HEAD_DIM, GQA_KV_HEADS * HEAD_DIM, GQA_KV_HEADS * HEAD_DIM,
    2 * DIFF_HEADS * DIFF_HEAD_DIM, 2 * DIFF_HEADS * DIFF_HEAD_DIM, 2 * DIFF_HEADS * DIFF_HEAD_DIM,
    RET_HEADS * RET_QK_DIM, RET_HEADS * RET_QK_DIM, RET_HEADS * RET_V_DIM, RET_HEADS * RET_V_DIM,
    3 * DN_HEADS * DN_HEAD_DIM, DN_HEADS * DN_HEAD_DIM, 2 * DN_HEADS, 2 * DN_HEADS,
    N_BRANCHES * D_MODEL,
)
D_IN = sum(IN_SIZES)

kernel_name = 'hybrid_parallel_mixer_dit_block'


def rms_normalize(x):
    xf = x.astype(jnp.float32)
    return xf * lax.rsqrt(jnp.mean(xf * xf, axis=-1, keepdims=True) + RMS_EPS)


def rms_norm(x, gain):
    return (rms_normalize(x) * gain.astype(jnp.float32)).astype(x.dtype)


def l2_normalize(x):
    xf = x.astype(jnp.float32)
    return xf * lax.rsqrt(jnp.sum(xf * xf, axis=-1, keepdims=True) + RMS_EPS)


def modulate(x, gain, shift, scale):
    return rms_norm(x, gain) * (1 + scale) + shift


def seq_flip(a, direction):
    return jnp.flip(a, axis=1) if direction == 1 else a


def axial_rope_angles(n_tok, head_dim):
    rows = n_tok // GRID_W
    row = jnp.repeat(jnp.arange(rows), GRID_W).astype(jnp.float32)
    col = jnp.tile(jnp.arange(GRID_W), rows).astype(jnp.float32)
    half = head_dim // 2
    inv = ROPE_THETA ** (-jnp.arange(0, half, 2, dtype=jnp.float32) / half)
    return jnp.concatenate([row[:, None] * inv, col[:, None] * inv], axis=-1)


def line_rope_angles(n_tok, head_dim):
    pos = jnp.arange(n_tok, dtype=jnp.float32)
    inv = ROPE_THETA ** (-jnp.arange(0, head_dim, 2, dtype=jnp.float32) / head_dim)
    return pos[:, None] * inv


def apply_rope(x, ang):
    extra = x.ndim - 3
    ang = ang.reshape(ang.shape[0], *([1] * extra), ang.shape[1])
    cos, sin = jnp.cos(ang), jnp.sin(ang)
    xf = x.astype(jnp.float32)
    x1, x2 = xf[..., 0::2], xf[..., 1::2]
    out = jnp.stack([x1 * cos - x2 * sin, x1 * sin + x2 * cos], axis=-1).reshape(x.shape)
    return out.astype(x.dtype)


def sweep_query_blocks(fn, *qs):
    b, s = qs[0].shape[:2]
    nb = s // Q_BLOCK

    def to_blocks(a):
        return jnp.moveaxis(a.reshape(b, nb, Q_BLOCK, *a.shape[2:]), 1, 0)

    out = lax.map(lambda blk: fn(*blk), tuple(to_blocks(a) for a in qs))
    return jnp.moveaxis(out, 0, 1).reshape(b, s, *out.shape[3:])


def gqa_core(q, k, v):
    s = jnp.einsum('bqhgd,bkhd->bhgqk', q, k).astype(jnp.float32) * (q.shape[-1] ** -0.5)
    p = jax.nn.softmax(s, axis=-1).astype(v.dtype)
    return jnp.einsum('bhgqk,bkhd->bqhgd', p, v)


def diff_core(q1, q2, k1, k2, v, lam):
    scale = q1.shape[-1] ** -0.5
    p1 = jax.nn.softmax(jnp.einsum('bqhd,bkhd->bhqk', q1, k1).astype(jnp.float32) * scale, axis=-1)
    p2 = jax.nn.softmax(jnp.einsum('bqhd,bkhd->bhqk', q2, k2).astype(jnp.float32) * scale, axis=-1)
    w = (p1 - lam * p2).astype(v.dtype)
    return jnp.einsum('bhqk,bkhe->bqhe', w, v)


def retention_scan(q, k, v, log_gamma, state0):
    b, l, h, _ = q.shape
    dv = v.shape[-1]
    n = l // RET_CHUNK
    idx = jnp.arange(RET_CHUNK, dtype=jnp.float32)
    rel = idx[:, None] - idx[None, :]
    intra = jnp.where(rel >= 0, jnp.exp(log_gamma[:, None, None] * jnp.maximum(rel, 0.0)), 0.0)
    q_decay = jnp.exp(log_gamma[None, :] * (idx[:, None] + 1.0))
    k_decay = jnp.exp(log_gamma[None, :] * (RET_CHUNK - 1.0 - idx[:, None]))
    chunk_decay = jnp.exp(log_gamma * RET_CHUNK)

    def chunks(a):
        return jnp.moveaxis(a.astype(jnp.float32).reshape(b, n, RET_CHUNK, *a.shape[2:]), 1, 0)

    def step(state, inp):
        qi, ki, vi = inp
        scores = jnp.einsum('bihd,bjhd->bhij', qi, ki) * intra
        o = (jnp.einsum('bhij,bjhe->bihe', scores, vi)
             + jnp.einsum('bihd,bhde->bihe', qi * q_decay[None, :, :, None], state))
        state = (state * chunk_decay[None, :, None, None]
                 + jnp.einsum('bjhd,bjhe->bhde', ki * k_decay[None, :, :, None], vi))
        return state, o

    state, o = lax.scan(step, state0, (chunks(q), chunks(k), chunks(v)))
    return jnp.moveaxis(o, 0, 1).reshape(b, l, h, dv), state


def gated_delta_scan(q, k, v, log_alpha, beta, state0):
    b, l, h, _ = q.shape
    dv = v.shape[-1]
    n = l // DN_CHUNK
    tri = jnp.tril(jnp.ones((DN_CHUNK, DN_CHUNK), bool))
    strict = jnp.tril(jnp.ones((DN_CHUNK, DN_CHUNK), bool), -1)
    eye = jnp.eye(DN_CHUNK, dtype=jnp.float32)

    def chunks(a):
        a = a.astype(jnp.float32).reshape(b, n, DN_CHUNK, h, *a.shape[3:])
        return jnp.moveaxis(jnp.moveaxis(a, 1, 0), 3, 2)

    def step(state, inp):
        qi, ki, vi, gi, bi = inp
        gc = jnp.cumsum(gi, axis=-1)
        decay = jnp.exp(jnp.where(tri, gc[..., :, None] - gc[..., None, :], -jnp.inf))
        kb = ki * bi[..., None]
        lower = jnp.where(strict, jnp.einsum('bhid,bhjd->bhij', kb, ki) * decay, 0.0)
        rhs = jnp.concatenate([vi * bi[..., None], kb * jnp.exp(gc)[..., None]], axis=-1)
        sol = lax.linalg.triangular_solve(eye + lower, rhs, left_side=True, lower=True)
        u, w = sol[..., :dv], sol[..., dv:]
        v_new = u - w @ state
        attn = jnp.einsum('bhid,bhjd->bhij', qi, ki) * decay
        o = (jnp.einsum('bhid,bhde->bhie', qi * jnp.exp(gc)[..., None], state)
             + jnp.einsum('bhij,bhje->bhie', attn, v_new))
        last = gc[..., -1:]
        state = (state * jnp.exp(last)[..., None]
                 + jnp.einsum('bhjd,bhje->bhde', ki * jnp.exp(last - gc)[..., None], v_new))
        return state, o

    state, o = lax.scan(step, state0, (chunks(q), chunks(k), chunks(v), chunks(log_alpha), chunks(beta)))
    o = jnp.swapaxes(jnp.moveaxis(o, 0, 1), 2, 3)
    return o.reshape(b, l, h, dv), state


def short_conv(x, w):
    pad = DN_CONV // 2
    return lax.conv_general_dilated(
        x, w[:, None, :].astype(x.dtype), window_strides=(1,), padding=[(pad, pad)],
        dimension_numbers=('NWC', 'WIO', 'NWC'), feature_group_count=x.shape[-1])


def gqa_branch(lat, ctx, qk_gain, ang, with_ctx):
    def heads(q, k, v):
        b, l = q.shape[:2]
        q = rms_norm(q.reshape(b, l, GQA_KV_HEADS, GQA_GROUP, HEAD_DIM), qk_gain[0])
        k = rms_norm(k.reshape(b, l, GQA_KV_HEADS, HEAD_DIM), qk_gain[1])
        return q, k, v.reshape(b, l, GQA_KV_HEADS, HEAD_DIM)

    ql, kl, vl = heads(*lat)
    qc, kc, vc = heads(*ctx)
    ql, kl = apply_rope(ql, ang), apply_rope(kl, ang)
    k_all = jnp.concatenate([kc, kl], axis=1)
    v_all = jnp.concatenate([vc, vl], axis=1)
    o_lat = sweep_query_blocks(lambda q: gqa_core(q, k_all, v_all), ql)
    o_lat = o_lat.reshape(*o_lat.shape[:2], BRANCH_WIDTH)
    o_ctx = gqa_core(qc, kc, vc).reshape(*qc.shape[:2], BRANCH_WIDTH) if with_ctx else None
    return o_lat, o_ctx


def diff_branch(lat, ctx, lam_params, norm_gain, ang, layer_idx, with_ctx):
    lam_init = 0.8 - 0.6 * math.exp(-0.3 * layer_idx)
    lp = lam_params.astype(jnp.float32)
    lam = jnp.exp(jnp.sum(lp[0] * lp[1])) - jnp.exp(jnp.sum(lp[2] * lp[3])) + lam_init

    def heads(q, k, v):
        b, l = q.shape[:2]
        return (q.reshape(b, l, DIFF_HEADS, 2, DIFF_HEAD_DIM),
                k.reshape(b, l, DIFF_HEADS, 2, DIFF_HEAD_DIM),
                v.reshape(b, l, DIFF_HEADS, 2 * DIFF_HEAD_DIM))

    def finish(o):
        b, l = o.shape[:2]
        return (rms_norm(o, norm_gain) * (1.0 - lam_init)).reshape(b, l, BRANCH_WIDTH)

    ql, kl, vl = heads(*lat)
    qc, kc, vc = heads(*ctx)
    ql, kl = apply_rope(ql, ang), apply_rope(kl, ang)
    k_all = jnp.concatenate([kc, kl], axis=1)
    v_all = jnp.concatenate([vc, vl], axis=1)
    k1_all, k2_all = k_all[..., 0, :], k_all[..., 1, :]
    o_lat = sweep_query_blocks(lambda q1, q2: diff_core(q1, q2, k1_all, k2_all, v_all, lam),
                               ql[..., 0, :], ql[..., 1, :])
    o_ctx = None
    if with_ctx:
        o_ctx = finish(diff_core(qc[..., 0, :], qc[..., 1, :], kc[..., 0, :], kc[..., 1, :], vc, lam))
    return finish(o_lat), o_ctx


def retention_branch(lat, ctx, decay_logit, ang, with_ctx):
    log_gamma = jax.nn.log_sigmoid(decay_logit.astype(jnp.float32))

    def heads(q, k, v, g):
        b, l = q.shape[:2]
        return (q.reshape(b, l, RET_HEADS, RET_QK_DIM),
                k.reshape(b, l, RET_HEADS, RET_QK_DIM) * (RET_QK_DIM ** -0.5),
                v.reshape(b, l, RET_HEADS, RET_V_DIM), g)

    def finish(o, g):
        b, l = g.shape[:2]
        gate = jax.nn.silu(g.astype(jnp.float32)).reshape(b, l, RET_HEADS, RET_V_DIM)
        return (rms_normalize(o) * gate).reshape(b, l, BRANCH_WIDTH).astype(g.dtype)

    ql, kl, vl, gl = heads(*lat)
    qc, kc, vc, gc = heads(*ctx)
    ql, kl = apply_rope(ql, ang), apply_rope(kl, ang)
    state0 = jnp.zeros((ql.shape[0], RET_HEADS, RET_QK_DIM, RET_V_DIM), jnp.float32)
    outs_lat, outs_ctx = [], []
    for d in range(2):
        oc, sc = retention_scan(seq_flip(qc, d), seq_flip(kc, d), seq_flip(vc, d), log_gamma[d], state0)
        ol, _ = retention_scan(seq_flip(ql, d), seq_flip(kl, d), seq_flip(vl, d), log_gamma[d], sc)
        outs_lat.append(seq_flip(ol, d))
        outs_ctx.append(seq_flip(oc, d))
    o_ctx = finish(outs_ctx[0] + outs_ctx[1], gc) if with_ctx else None
    return finish(outs_lat[0] + outs_lat[1], gl), o_ctx


def deltanet_branch(lat, ctx, conv_w, a_log, dt_bias, norm_gain, with_ctx):
    decay_rate = jnp.exp(a_log.astype(jnp.float32))

    def prep(qkv, z, beta_logit, a):
        b, l = qkv.shape[:2]
        qkv = jax.nn.silu(short_conv(qkv, conv_w))
        q, k, v = jnp.split(qkv, 3, axis=-1)
        q = l2_normalize(q.reshape(b, l, DN_HEADS, DN_HEAD_DIM)) * (DN_HEAD_DIM ** -0.5)
        k = l2_normalize(k.reshape(b, l, DN_HEADS, DN_HEAD_DIM))
        v = v.reshape(b, l, DN_HEADS, DN_HEAD_DIM)
        beta = jax.nn.sigmoid(beta_logit.astype(jnp.float32)).reshape(b, l, 2, DN_HEADS)
        log_alpha = -decay_rate * jax.nn.softplus(
            a.astype(jnp.float32).reshape(b, l, 2, DN_HEADS) + dt_bias.astype(jnp.float32))
        return q, k, v, log_alpha, beta, z

    def finish(o, z):
        b, l = z.shape[:2]
        gate = jax.nn.silu(z.astype(jnp.float32)).reshape(b, l, DN_HEADS, DN_HEAD_DIM)
        return (rms_norm(o, norm_gain) * gate).reshape(b, l, BRANCH_WIDTH).astype(z.dtype)

    ql, kl, vl, lal, bl, zl = prep(*lat)
    qc, kc, vc, lac, bc, zc = prep(*ctx)
    state0 = jnp.zeros((ql.shape[0], DN_HEADS, DN_HEAD_DIM, DN_HEAD_DIM), jnp.float32)
    outs_lat, outs_ctx = [], []
    for d in range(2):
        oc, sc = gated_delta_scan(seq_flip(qc, d), seq_flip(kc, d), seq_flip(vc, d),
                                  seq_flip(lac[:, :, d], d), seq_flip(bc[:, :, d], d), state0)
        ol, _ = gated_delta_scan(seq_flip(ql, d), seq_flip(kl, d), seq_flip(vl, d),
                                 seq_flip(lal[:, :, d], d), seq_flip(bl[:, :, d], d), sc)
        outs_lat.append(seq_flip(ol, d))
        outs_ctx.append(seq_flip(oc, d))
    o_ctx = finish(outs_ctx[0] + outs_ctx[1], zc) if with_ctx else None
    return finish(outs_lat[0] + outs_lat[1], zl), o_ctx


def merge_branches(outs, gate_logits, w_branch, w_out):
    gates = jnp.split(jax.nn.sigmoid(gate_logits.astype(jnp.float32)).astype(outs[0].dtype), N_BRANCHES, axis=-1)
    merged = sum(g * (o @ w_branch[i]) for i, (g, o) in enumerate(zip(gates, outs)))
    return merged @ w_out


def hybrid_mixer(h, hc, p, layer_idx, with_ctx):
    s = h.shape[1]
    cuts = np.cumsum(IN_SIZES)[:-1].tolist()
    ul = jnp.split(h @ p['w_in'], cuts, axis=-1)
    uc = jnp.split(hc @ p['w_in'], cuts, axis=-1)
    axial = axial_rope_angles(s, HEAD_DIM)
    line = line_rope_angles(s, RET_QK_DIM)
    br_a = gqa_branch(ul[0:3], uc[0:3], p['gqa_qk_gain'], axial, with_ctx)
    br_b = diff_branch(ul[3:6], uc[3:6], p['diff_lambda'], p['diff_norm'], axial, layer_idx, with_ctx)
    br_c = retention_branch(ul[6:10], uc[6:10], p['ret_decay_logit'], line, with_ctx)
    br_d = deltanet_branch(ul[10:14], uc[10:14], p['dn_conv_w'], p['dn_a_log'], p['dn_dt_bias'],
                           p['dn_norm'], with_ctx)
    y = merge_branches([br_a[0], br_b[0], br_c[0], br_d[0]], ul[14], p['w_branch'], p['w_out'])
    yc = None
    if with_ctx:
        yc = merge_branches([br_a[1], br_b[1], br_c[1], br_d[1]], uc[14], p['w_branch'], p['w_out'])
    return y, yc


def moe_ffn(h, w_router, b_router, w_gate_up, b_gate_up, w_down, b_down):
    n, d = h.shape
    logits = (h @ w_router + b_router).astype(jnp.float32)
    top_val, top_idx = lax.top_k(logits, TOP_K)
    weights = jax.nn.softmax(top_val, axis=-1)
    flat_e = top_idx.reshape(-1)
    flat_tok = jnp.repeat(jnp.arange(n), TOP_K)
    order = jnp.argsort(flat_e)
    sorted_e = flat_e[order]
    tok_sorted = flat_tok[order]
    counts = jnp.bincount(flat_e, length=N_EXPERTS)
    starts = jnp.cumsum(counts) - counts
    padded = (counts + MOE_BLOCK - 1) // MOE_BLOCK * MOE_BLOCK
    pad_ends = jnp.cumsum(padded)
    pad_starts = pad_ends - padded
    dest = pad_starts[sorted_e] + jnp.arange(n * TOP_K) - starts[sorted_e]
    n_blocks = (n * TOP_K) // MOE_BLOCK + N_EXPERTS
    buf = jnp.zeros((n_blocks * MOE_BLOCK, d), h.dtype).at[dest].set(h[tok_sorted])
    block_e = jnp.minimum(jnp.searchsorted(pad_ends, jnp.arange(n_blocks) * MOE_BLOCK, side='right'),
                          N_EXPERTS - 1)

    def expert_block(xb, e):
        gu = xb @ w_gate_up[e] + b_gate_up[e]
        gate, up = jnp.split(gu, 2, axis=-1)
        gate = jnp.minimum(gate, SWIGLU_LIMIT)
        up = jnp.clip(up, -SWIGLU_LIMIT, SWIGLU_LIMIT)
        act = (up + 1) * gate * jax.nn.sigmoid(SWIGLU_ALPHA * gate)
        return act @ w_down[e] + b_down[e]

    out = lax.map(lambda a: expert_block(*a), (buf.reshape(n_blocks, MOE_BLOCK, d), block_e))
    y_assign = out.reshape(-1, d)[dest] * weights.reshape(-1)[order][:, None].astype(out.dtype)
    return jax.ops.segment_sum(y_assign, tok_sorted, num_segments=n)


def trunk_layer(x, xc, c, c_ctx, p, layer_idx, with_ctx):
    mod = jax.nn.silu(c) @ p['w_mod'] + p['b_mod']
    sh1, sc1, g1, sh2, sc2, g2 = jnp.split(mod[:, None, :], N_MOD, axis=-1)
    mod_c = jax.nn.silu(c_ctx) @ p['w_mod'] + p['b_mod']
    csh1, csc1, cg1, csh2, csc2, cg2 = jnp.split(mod_c, N_MOD)
    gain = p['norm_gain']
    y, yc = hybrid_mixer(modulate(x, gain[0], sh1, sc1), modulate(xc, gain[0], csh1, csc1),
                         p, layer_idx, with_ctx)
    x = x + g1 * rms_norm(y, gain[1])
    b, s, d = x.shape
    h2 = modulate(x, gain[2], sh2, sc2).reshape(b * s, d)
    moe_w = (p['w_router'], p['b_router'], p['w_gate_up'], p['b_gate_up'], p['w_down'], p['b_down'])
    if with_ctx:
        xc = xc + cg1 * rms_norm(yc, gain[1])
        h2c = modulate(xc, gain[2], csh2, csc2).reshape(-1, d)
        f = moe_ffn(jnp.concatenate([h2, h2c], axis=0), *moe_w)
        x = x + g2 * rms_norm(f[:b * s].reshape(b, s, d), gain[3])
        xc = xc + cg2 * rms_norm(f[b * s:].reshape(xc.shape), gain[3])
    else:
        x = x + g2 * rms_norm(moe_ffn(h2, *moe_w).reshape(b, s, d), gain[3])
    return x, xc


def setup_inputs(seed: int = 0) -> dict:
    key = jax.random.key(seed)
    ks = jax.random.split(key, 24)
    f32 = jnp.float32

    def nrm(k, shape, scale):
        return jax.random.normal(k, shape, f32) * scale

    gam = 1.0 - 2.0 ** (-5.0 - jnp.arange(RET_HEADS, dtype=f32))
    dt = jnp.exp(jax.random.uniform(ks[14], (DEPTH, 2, DN_HEADS), f32, math.log(1e-3), math.log(1e-1)))
    return {
        'x': nrm(ks[0], (BATCH, SEQ, D_MODEL), 1.0),
        'c': nrm(ks[1], (BATCH, D_MODEL), 1.0),
        'ctx': nrm(ks[2], (BATCH, CTX_LEN, D_MODEL), 1.0),
        'c_ctx': nrm(ks[3], (D_MODEL,), 1.0),
        'w_mod': nrm(ks[4], (DEPTH, D_MODEL, N_MOD * D_MODEL), 0.5 * D_MODEL ** -0.5),
        'b_mod': nrm(ks[5], (DEPTH, N_MOD * D_MODEL), 0.02),
        'norm_gain': 1.0 + nrm(ks[6], (DEPTH, 4, D_MODEL), 0.02),
        'w_in': nrm(ks[7], (DEPTH, D_MODEL, D_IN), D_MODEL ** -0.5),
        'gqa_qk_gain': 1.0 + nrm(ks[8], (DEPTH, 2, HEAD_DIM), 0.02),
        'diff_lambda': nrm(ks[9], (DEPTH, 4, DIFF_HEAD_DIM), 0.1),
        'diff_norm': 1.0 + nrm(ks[10], (DEPTH, 2 * DIFF_HEAD_DIM), 0.02),
        'ret_decay_logit': jnp.log(gam / (1.0 - gam))[None, None, :] + nrm(ks[11], (DEPTH, 2, RET_HEADS), 0.05),
        'dn_conv_w': nrm(ks[12], (DEPTH, DN_CONV, 3 * DN_HEADS * DN_HEAD_DIM), DN_CONV ** -0.5),
        'dn_a_log': jnp.log(jax.random.uniform(ks[13], (DEPTH, 2, DN_HEADS), f32, 1.0, 16.0)),
        'dn_dt_bias': dt + jnp.log(-jnp.expm1(-dt)),
        'dn_norm': 1.0 + nrm(ks[15], (DEPTH, DN_HEAD_DIM), 0.02),
        'w_branch': nrm(ks[16], (DEPTH, N_BRANCHES, BRANCH_WIDTH, D_MODEL), BRANCH_WIDTH ** -0.5),
        'w_out': nrm(ks[17], (DEPTH, D_MODEL, D_MODEL), D_MODEL ** -0.5),
        'w_router': nrm(ks[18], (DEPTH, D_MODEL, N_EXPERTS), D_MODEL ** -0.5),
        'b_router': nrm(ks[19], (DEPTH, N_EXPERTS), 0.01),
        'w_gate_up': nrm(ks[20], (DEPTH, N_EXPERTS, D_MODEL, 2 * EXPERT_FF), D_MODEL ** -0.5),
        'b_gate_up': nrm(ks[21], (DEPTH, N_EXPERTS, 2 * EXPERT_FF), 0.01),
        'w_down': nrm(ks[22], (DEPTH, N_EXPERTS, EXPERT_FF, D_MODEL), EXPERT_FF ** -0.5),
        'b_down': nrm(ks[23], (DEPTH, N_EXPERTS, D_MODEL), 0.01),
    }


def reference(x, c, ctx, c_ctx, w_mod, b_mod, norm_gain, w_in, gqa_qk_gain, diff_lambda, diff_norm,
              ret_decay_logit, dn_conv_w, dn_a_log, dn_dt_bias, dn_norm, w_branch, w_out,
              w_router, b_router, w_gate_up, b_gate_up, w_down, b_down):
    xc = ctx
    for l in range(DEPTH):
        p = {
            'w_mod': w_mod[l], 'b_mod': b_mod[l], 'norm_gain': norm_gain[l], 'w_in': w_in[l],
            'gqa_qk_gain': gqa_qk_gain[l], 'diff_lambda': diff_lambda[l], 'diff_norm': diff_norm[l],
            'ret_decay_logit': ret_decay_logit[l], 'dn_conv_w': dn_conv_w[l], 'dn_a_log': dn_a_log[l],
            'dn_dt_bias': dn_dt_bias[l], 'dn_norm': dn_norm[l], 'w_branch': w_branch[l], 'w_out': w_out[l],
            'w_router': w_router[l], 'b_router': b_router[l], 'w_gate_up': w_gate_up[l],
            'b_gate_up': b_gate_up[l], 'w_down': w_down[l], 'b_down': b_down[l],
        }
        x, xc = trunk_layer(x, xc, c, c_ctx, p, l, l < DEPTH - 1)
    return x
```

```python
import functools
import math

import numpy as np
import jax
import jax.numpy as jnp
from jax import lax
from jax.experimental import pallas as pl
from jax.experimental.pallas import tpu as pltpu

F32 = jnp.float32
BF16 = jnp.bfloat16

GRID_W = 64
RMS_EPS = 1e-6
ROPE_THETA = 10000.0
HEAD_DIM = 64
GQA_KV_HEADS = 2
GQA_GROUP = 4
DIFF_HEADS = 4
RET_HEADS = 4
RET_QK_DIM = 64
RET_V_DIM = 128
DN_HEADS = 4
DN_HEAD_DIM = 128
DN_CONV = 5
DN_CHUNK = 64
TOP_K = 4
SWIGLU_LIMIT = 7.0
SWIGLU_ALPHA = 1.702
N_MOD = 6
N_BRANCHES = 4

LANES = 128
SUBLANES = 8
ROW_TILE = 256
RET_CHUNK = 128
MOE_TILE = 256
VMEM_LIMIT = 56 * 1024 * 1024


def _cparams(*sem):
    return pltpu.CompilerParams(dimension_semantics=sem, vmem_limit_bytes=VMEM_LIMIT)


def _const_spec(shape):
    nd = len(shape)
    return pl.BlockSpec(shape, lambda *_: (0,) * nd)


def _rms(x):
    return x * lax.rsqrt(jnp.mean(x * x, axis=-1, keepdims=True) + RMS_EPS)


def _silu(x):
    return x * jax.nn.sigmoid(x)


def _dot(a, b):
    return jnp.dot(a, b, preferred_element_type=F32)


def _dot_nt(a, b):
    return lax.dot_general(a, b, (((1,), (1,)), ((), ())), preferred_element_type=F32)


def _dot_tn(a, b):
    return lax.dot_general(a, b, (((0,), (0,)), ((), ())), preferred_element_type=F32)


def _dot_hi(a, b):
    return jnp.dot(a, b, preferred_element_type=F32, precision=lax.Precision.HIGHEST)


def _dot_exact_lhs(a01, x):
    a = a01.astype(BF16)
    x1 = x.astype(BF16)
    r1 = x - x1.astype(F32)
    x2 = r1.astype(BF16)
    x3 = (r1 - x2.astype(F32)).astype(BF16)
    return _dot(a, x1) + _dot(a, x2) + _dot(a, x3)


def _mod_kernel(c_ref, w_ref, b_ref, o_ref):
    o_ref[...] = _dot(_silu(c_ref[...]), w_ref[...]) + b_ref[...]


def _mod_table(c_all, w_mod, b_mod):
    m, d = c_all.shape
    n = w_mod.shape[1]
    tn = d
    return pl.pallas_call(
        _mod_kernel,
        out_shape=jax.ShapeDtypeStruct((m, n), F32),
        grid=(n // tn,),
        in_specs=[pl.BlockSpec((m, d), lambda j: (0, 0)),
                  pl.BlockSpec((d, tn), lambda j: (0, j)),
                  pl.BlockSpec((1, tn), lambda j: (0, j))],
        out_specs=pl.BlockSpec((m, tn), lambda j: (0, j)),
        compiler_params=_cparams("parallel"),
        name="mod_table",
    )(c_all, w_mod, b_mod.reshape(1, n))


def _modulate_kernel(x_ref, mod_ref, gain_ref, h_ref):
    xn = _rms(x_ref[...]) * gain_ref[...]
    h = xn * (1.0 + mod_ref[1:2, :]) + mod_ref[0:1, :]
    h_ref[...] = h.astype(h_ref.dtype)


def _modulate(x, modtab, gain_row, nct):
    b, t, d = x.shape
    return pl.pallas_call(
        _modulate_kernel,
        out_shape=jax.ShapeDtypeStruct((b, t, d), BF16),
        grid=(b, t // ROW_TILE),
        in_specs=[pl.BlockSpec((None, ROW_TILE, d), lambda i, j: (i, j, 0)),
                  pl.BlockSpec((None, None, N_MOD, d), lambda i, j: (i, (j >= nct).astype(jnp.int32), 0, 0)),
                  pl.BlockSpec((1, d), lambda i, j: (0, 0))],
        out_specs=pl.BlockSpec((None, ROW_TILE, d), lambda i, j: (i, j, 0)),
        compiler_params=_cparams("parallel", "parallel"),
        name="modulate",
    )(x, modtab, gain_row)


def _swap_halves(y):
    lane = lax.broadcasted_iota(jnp.int32, y.shape, 1)
    first = (lane % HEAD_DIM) < (HEAD_DIM // 2)
    return jnp.where(first, pltpu.roll(y, LANES - HEAD_DIM // 2, 1), pltpu.roll(y, HEAD_DIM // 2, 1))


def _inproj_kernel(h_ref, w_ref, cos_ref, sin_ref, gain_ref, seg_ref, *out_refs, widths, n_rope, do_norm):
    u = _dot(h_ref[...], w_ref[...])
    cos = cos_ref[...]
    sin = sin_ref[...]
    col = 0
    for o_ref, wd in zip(out_refs, widths):
        for j in range(0, wd, LANES):
            wj = min(LANES, wd - j)
            y = u[:, col + j: col + j + wj]
            if col + j < n_rope:
                if do_norm:
                    ms = _dot((y * y).astype(BF16), seg_ref[...])
                    y = y * lax.rsqrt(ms + RMS_EPS) * gain_ref[:, col + j: col + j + wj]
                y = y * cos + _swap_halves(y) * sin
            o_ref[:, j: j + wj] = y.astype(o_ref.dtype)
        col += wd


def _inproj(h, w, cos, sin, gain_row, outs, n_rope, do_norm):
    b, t, d = h.shape
    c = w.shape[1]
    widths = tuple(o[0] for o in outs)
    assert sum(widths) == c
    seg = np.kron(np.eye(LANES // HEAD_DIM), np.full((HEAD_DIM, HEAD_DIM), 1.0 / HEAD_DIM))
    seg = jnp.asarray(seg, BF16)
    kern = functools.partial(_inproj_kernel, widths=widths, n_rope=n_rope, do_norm=do_norm)
    return pl.pallas_call(
        kern,
        out_shape=[jax.ShapeDtypeStruct((b, t, wd), dt) for wd, dt in outs],
        grid=(b, t // ROW_TILE),
        in_specs=[pl.BlockSpec((None, ROW_TILE, d), lambda i, j: (i, j, 0)),
                  pl.BlockSpec((d, c), lambda i, j: (0, 0)),
                  pl.BlockSpec((ROW_TILE, LANES), lambda i, j: (j, 0)),
                  pl.BlockSpec((ROW_TILE, LANES), lambda i, j: (j, 0)),
                  pl.BlockSpec(gain_row.shape, lambda i, j: (0, 0)),
                  pl.BlockSpec((LANES, LANES), lambda i, j: (0, 0))],
        out_specs=[pl.BlockSpec((None, ROW_TILE, wd), lambda i, j: (i, j, 0)) for wd, _ in outs],
        compiler_params=_cparams("parallel", "parallel"),
        name="inproj",
    )(h, w, cos, sin, gain_row, seg)


def _softmax_pv(s, v):
    m = jnp.max(s, axis=-1, keepdims=True)
    p = jnp.exp(s - m)
    l = jnp.sum(p, axis=-1, keepdims=True)
    return _dot(p.astype(BF16), v) / l


def _gqa_kernel(q_ref, k_ref, v_ref, o_ref, *, nct, ctx_len, tile0):
    t = pl.program_id(1) + tile0
    scale = HEAD_DIM ** -0.5

    def run(nk):
        for kv in range(GQA_KV_HEADS):
            kh = k_ref[0:nk, kv * HEAD_DIM:(kv + 1) * HEAD_DIM]
            vh = v_ref[0:nk, kv * HEAD_DIM:(kv + 1) * HEAD_DIM]
            for g in range(GQA_GROUP):
                c0 = (kv * GQA_GROUP + g) * HEAD_DIM
                s = _dot_nt(q_ref[:, c0:c0 + HEAD_DIM], kh) * scale
                o_ref[:, c0:c0 + HEAD_DIM] = _softmax_pv(s, vh).astype(o_ref.dtype)

    if tile0 < nct:
        @pl.when(t < nct)
        def _():
            run(ctx_len)

    @pl.when(t >= nct)
    def _():
        run(k_ref.shape[0])


def _gqa_attention(q, k, v, nct, ctx_len, tile0):
    b, t, cq = q.shape
    nt = t // ROW_TILE - tile0
    kern = functools.partial(_gqa_kernel, nct=nct, ctx_len=ctx_len, tile0=tile0)
    return pl.pallas_call(
        kern,
        out_shape=jax.ShapeDtypeStruct((b, t, cq), BF16),
        grid=(b, nt),
        in_specs=[pl.BlockSpec((None, ROW_TILE, cq), lambda i, j: (i, j + tile0, 0)),
                  pl.BlockSpec((None, t, k.shape[2]), lambda i, j: (i, 0, 0)),
                  pl.BlockSpec((None, t, v.shape[2]), lambda i, j: (i, 0, 0))],
        out_specs=pl.BlockSpec((None, ROW_TILE, cq), lambda i, j: (i, j + tile0, 0)),
        compiler_params=_cparams("parallel", "parallel"),
        name="gqa_attention",
    )(q, k, v)


def _diff_kernel(q_ref, k_ref, v_ref, lam_ref, gain_ref, o_ref, *, nct, ctx_len, tile0, lam_init):
    t = pl.program_id(1) + tile0
    scale = HEAD_DIM ** -0.5
    lp = lam_ref[...]
    lam = (jnp.exp(jnp.sum(lp[0:1, :] * lp[1:2, :], axis=-1, keepdims=True))
           - jnp.exp(jnp.sum(lp[2:3, :] * lp[3:4, :], axis=-1, keepdims=True)) + lam_init)
    dv = 2 * HEAD_DIM

    def run(nk):
        for hd in range(DIFF_HEADS):
            c0 = hd * dv
            vh = v_ref[0:nk, c0:c0 + dv]
            s1 = _dot_nt(q_ref[:, c0:c0 + HEAD_DIM], k_ref[0:nk, c0:c0 + HEAD_DIM]) * scale
            o1 = _softmax_pv(s1, vh)
            s2 = _dot_nt(q_ref[:, c0 + HEAD_DIM:c0 + dv], k_ref[0:nk, c0 + HEAD_DIM:c0 + dv]) * scale
            o2 = _softmax_pv(s2, vh)
            o = _rms(o1 - lam * o2) * gain_ref[...] * (1.0 - lam_init)
            o_ref[:, c0:c0 + dv] = o.astype(o_ref.dtype)

    if tile0 < nct:
        @pl.when(t < nct)
        def _():
            run(ctx_len)

    @pl.when(t >= nct)
    def _():
        run(k_ref.shape[0])


def _diff_attention(q, k, v, lam_params, gain_row, nct, ctx_len, tile0, lam_init):
    b, t, cq = q.shape
    nt = t // ROW_TILE - tile0
    kern = functools.partial(_diff_kernel, nct=nct, ctx_len=ctx_len, tile0=tile0, lam_init=lam_init)
    return pl.pallas_call(
        kern,
        out_shape=jax.ShapeDtypeStruct((b, t, cq), BF16),
        grid=(b, nt),
        in_specs=[pl.BlockSpec((None, ROW_TILE, cq), lambda i, j: (i, j + tile0, 0)),
                  pl.BlockSpec((None, t, cq), lambda i, j: (i, 0, 0)),
                  pl.BlockSpec((None, t, cq), lambda i, j: (i, 0, 0)),
                  pl.BlockSpec(lam_params.shape, lambda i, j: (0, 0)),
                  pl.BlockSpec(gain_row.shape, lambda i, j: (0, 0))],
        out_specs=pl.BlockSpec((None, ROW_TILE, cq), lambda i, j: (i, j + tile0, 0)),
        compiler_params=_cparams("parallel", "parallel"),
        name="diff_attention",
    )(q, k, v, lam_params, gain_row)


def _scan_chunk(d, s, n_ctx_chunks, n_chunks):
    rev = jnp.where(s < n_ctx_chunks, n_ctx_chunks - 1 - s, n_chunks + n_ctx_chunks - 1 - s)
    return jnp.where(d == 0, s, rev)


def _ret_kernel(lg_ref, q_ref, k_ref, v_ref, o_ref, state_ref):
    d = pl.program_id(1)
    s = pl.program_id(2)
    c = RET_CHUNK

    @pl.when(s == 0)
    def _():
        state_ref[...] = jnp.zeros_like(state_ref)

    fwd = d == 0
    i = lax.broadcasted_iota(jnp.int32, (c, c), 0)
    j = lax.broadcasted_iota(jnp.int32, (c, c), 1)
    rel = jnp.where(fwd, i - j, j - i)
    relf = jnp.maximum(rel, 0).astype(F32)
    r = lax.broadcasted_iota(jnp.int32, (c, 1), 0)
    pos_q = jnp.where(fwd, r + 1, c - r).astype(F32)
    pos_k = jnp.where(fwd, c - 1 - r, r).astype(F32)
    for hd in range(RET_HEADS):
        lg = lg_ref[d, hd]
        intra = jnp.where(rel >= 0, jnp.exp(lg * relf), 0.0)
        qh = q_ref[:, hd * RET_QK_DIM:(hd + 1) * RET_QK_DIM]
        kh = k_ref[:, hd * RET_QK_DIM:(hd + 1) * RET_QK_DIM]
        vh = v_ref[:, hd * RET_V_DIM:(hd + 1) * RET_V_DIM]
        st = state_ref[hd]
        scores = _dot_nt(qh, kh) * intra
        qd = (qh.astype(F32) * jnp.exp(lg * pos_q)).astype(BF16)
        o = _dot(scores.astype(BF16), vh) + _dot(qd, st.astype(BF16))
        kd = (kh.astype(F32) * jnp.exp(lg * pos_k)).astype(BF16)
        state_ref[hd] = st * jnp.exp(lg * jnp.full((1, 1), c, F32)) + _dot_tn(kd, vh)
        o_ref[:, hd * RET_V_DIM:(hd + 1) * RET_V_DIM] = o


def _retention(q, k, v, log_gamma, ctx_len):
    b, t, _ = q.shape
    n = t // RET_CHUNK
    nc = ctx_len // RET_CHUNK

    def rows(i, d, s):
        return (i, _scan_chunk(d, s, nc, n), 0)

    return pl.pallas_call(
        _ret_kernel,
        out_shape=jax.ShapeDtypeStruct((2, b, t, v.shape[2]), F32),
        grid=(b, 2, n),
        in_specs=[pl.BlockSpec(memory_space=pltpu.SMEM),
                  pl.BlockSpec((None, RET_CHUNK, q.shape[2]), rows),
                  pl.BlockSpec((None, RET_CHUNK, k.shape[2]), rows),
                  pl.BlockSpec((None, RET_CHUNK, v.shape[2]), rows)],
        out_specs=pl.BlockSpec((None, None, RET_CHUNK, v.shape[2]),
                               lambda i, d, s: (d, i, _scan_chunk(d, s, nc, n), 0)),
        scratch_shapes=[pltpu.VMEM((RET_HEADS, RET_QK_DIM, RET_V_DIM), F32)],
        compiler_params=_cparams("parallel", "arbitrary", "arbitrary"),
        name="retention_scan",
    )(log_gamma, q, k, v)


def _dn_prep_kernel(prev_ref, cur_ref, next_ref, cw_ref, ba_ref, rate_ref, bias_ref,
                    q_ref, k_ref, v_ref, g_ref, beta_ref, *, nct, n_tiles):
    t = pl.program_id(1)
    pad = DN_CONV // 2
    first = (t == 0) | (t == nct)
    last = (t == nct - 1) | (t == n_tiles - 1)
    prev = jnp.where(first, 0.0, prev_ref[...])
    nxt = jnp.where(last, 0.0, next_ref[...])
    ext = jnp.concatenate([prev, cur_ref[...], nxt], axis=0)
    rows = ext.shape[0]
    acc = None
    for tap in range(DN_CONV):
        sh = (pad - tap) % rows
        x = ext if sh == 0 else pltpu.roll(ext, sh, 0)
        term = x[SUBLANES:SUBLANES + ROW_TILE, :] * cw_ref[tap:tap + 1, :]
        acc = term if acc is None else acc + term
    y = _silu(acc)
    hw = DN_HEADS * DN_HEAD_DIM
    for hd in range(DN_HEADS):
        sl = slice(hd * DN_HEAD_DIM, (hd + 1) * DN_HEAD_DIM)
        qh = y[:, sl]
        kh = y[:, hw + hd * DN_HEAD_DIM: hw + (hd + 1) * DN_HEAD_DIM]
        q_ref[:, sl] = qh * lax.rsqrt(jnp.sum(qh * qh, axis=-1, keepdims=True) + RMS_EPS) * (DN_HEAD_DIM ** -0.5)
        k_ref[:, sl] = kh * lax.rsqrt(jnp.sum(kh * kh, axis=-1, keepdims=True) + RMS_EPS)
    v_ref[...] = y[:, 2 * hw:3 * hw]
    nb = 2 * DN_HEADS
    ba = ba_ref[...]
    beta_ref[...] = jax.nn.sigmoid(ba[:, 0:nb])
    a = ba[:, nb:2 * nb] + bias_ref[...]
    softplus = jnp.maximum(a, 0.0) + jnp.log(1.0 + jnp.exp(-jnp.abs(a)))
    g_ref[...] = -rate_ref[...] * softplus


def _dn_prep(qkv, ba, conv_w, rate_row, bias_row, nct):
    b, t, c = qkv.shape
    n_tiles = t // ROW_TILE
    hb = ROW_TILE // SUBLANES
    last_blk = t // SUBLANES - 1
    hw = DN_HEADS * DN_HEAD_DIM
    nb = 2 * DN_HEADS
    kern = functools.partial(_dn_prep_kernel, nct=nct, n_tiles=n_tiles)
    row_spec = lambda wd: pl.BlockSpec((None, ROW_TILE, wd), lambda i, j: (i, j, 0))
    return pl.pallas_call(
        kern,
        out_shape=[jax.ShapeDtypeStruct((b, t, hw), F32)] * 3 + [jax.ShapeDtypeStruct((b, t, nb), F32)] * 2,
        grid=(b, n_tiles),
        in_specs=[pl.BlockSpec((None, SUBLANES, c), lambda i, j: (i, jnp.maximum(j * hb - 1, 0), 0)),
                  row_spec(c),
                  pl.BlockSpec((None, SUBLANES, c), lambda i, j: (i, jnp.minimum((j + 1) * hb, last_blk), 0)),
                  pl.BlockSpec(conv_w.shape, lambda i, j: (0, 0)),
                  row_spec(2 * nb),
                  pl.BlockSpec((1, nb), lambda i, j: (0, 0)),
                  pl.BlockSpec((1, nb), lambda i, j: (0, 0))],
        out_specs=[row_spec(hw)] * 3 + [row_spec(nb)] * 2,
        compiler_params=_cparams("parallel", "parallel"),
        name="deltanet_prep",
    )(qkv, qkv, qkv, conv_w, ba, rate_row, bias_row)


def _dn_kernel(q_ref, k_ref, v_ref, g_ref, beta_ref, o_ref, state_ref):
    d = pl.program_id(1)
    s = pl.program_id(2)
    c = DN_CHUNK
    dh = DN_HEAD_DIM

    @pl.when(s == 0)
    def _():
        state_ref[...] = jnp.zeros_like(state_ref)

    fwd = d == 0
    i = lax.broadcasted_iota(jnp.int32, (c, c), 0)
    j = lax.broadcasted_iota(jnp.int32, (c, c), 1)
    rel = jnp.where(fwd, i - j, j - i)
    incl = rel >= 0
    strict = rel > 0
    tri = incl.astype(F32)
    eye = (i == j).astype(F32)
    g_all = g_ref[...]
    b_all = beta_ref[...]
    for hd in range(DN_HEADS):
        g = jnp.where(fwd, g_all[:, hd:hd + 1], g_all[:, DN_HEADS + hd:DN_HEADS + hd + 1])
        beta = jnp.where(fwd, b_all[:, hd:hd + 1], b_all[:, DN_HEADS + hd:DN_HEADS + hd + 1])
        gc = _dot_exact_lhs(tri, jnp.broadcast_to(g, (c, dh)))
        dmat = _dot_exact_lhs(tri, jnp.where(strict, jnp.broadcast_to(g, (c, c)), 0.0))
        decay = jnp.where(incl, jnp.exp(dmat), 0.0)
        sl = slice(hd * dh, (hd + 1) * dh)
        q = q_ref[:, sl]
        k = k_ref[:, sl]
        v = v_ref[:, sl]
        kb = k * beta
        kbf = k.astype(BF16)
        lower = jnp.where(strict, _dot_nt(kb.astype(BF16), kbf) * decay, 0.0)
        a = -lower
        tinv = eye + a
        p = a
        for _ in range(5):
            p = _dot_hi(p, p)
            tinv = tinv + _dot_hi(tinv, p)
        egc = jnp.exp(gc)
        u = _dot_hi(tinv, v * beta)
        w = _dot_hi(tinv, kb * egc)
        st = state_ref[hd]
        stb = st.astype(BF16)
        v_new = u - _dot(w.astype(BF16), stb)
        attn = jnp.where(incl, _dot_nt(q.astype(BF16), kbf) * decay, 0.0)
        vnb = v_new.astype(BF16)
        o = _dot((q * egc).astype(BF16), stb) + _dot(attn.astype(BF16), vnb)
        last = jnp.where(fwd, gc[c - 1:c, :], gc[0:1, :])
        kd = (k * jnp.exp(last - gc)).astype(BF16)
        state_ref[hd] = st * jnp.exp(last) + _dot_tn(kd, vnb)
        o_ref[:, sl] = o


def _deltanet(q, k, v, g, beta, ctx_len):
    b, t, hw = q.shape
    n = t // DN_CHUNK
    nc = ctx_len // DN_CHUNK

    def rows(i, d, s):
        return (i, _scan_chunk(d, s, nc, n), 0)

    return pl.pallas_call(
        _dn_kernel,
        out_shape=jax.ShapeDtypeStruct((2, b, t, hw), F32),
        grid=(b, 2, n),
        in_specs=[pl.BlockSpec((None, DN_CHUNK, hw), rows)] * 3
                 + [pl.BlockSpec((None, DN_CHUNK, g.shape[2]), rows)] * 2,
        out_specs=pl.BlockSpec((None, None, DN_CHUNK, hw),
                               lambda i, d, s: (d, i, _scan_chunk(d, s, nc, n), 0)),
        scratch_shapes=[pltpu.VMEM((DN_HEADS, DN_HEAD_DIM, DN_HEAD_DIM), F32)],
        compiler_params=_cparams("parallel", "arbitrary", "arbitrary"),
        name="deltanet_scan",
    )(q, k, v, g, beta)


def _gated_norm_kernel(o_ref, z_ref, gain_ref, out_ref):
    o = o_ref[0] + o_ref[1]
    z = z_ref[...].astype(F32)
    for j in range(0, o.shape[1], LANES):
        oh = _rms(o[:, j:j + LANES]) * gain_ref[...]
        out_ref[:, j:j + LANES] = (oh * _silu(z[:, j:j + LANES])).astype(out_ref.dtype)


def _gated_norm(o2, z, gain_row):
    _, b, t, w = o2.shape
    return pl.pallas_call(
        _gated_norm_kernel,
        out_shape=jax.ShapeDtypeStruct((b, t, w), BF16),
        grid=(b, t // ROW_TILE),
        in_specs=[pl.BlockSpec((2, None, ROW_TILE, w), lambda i, j: (0, i, j, 0)),
                  pl.BlockSpec((None, ROW_TILE, w), lambda i, j: (i, j, 0)),
                  pl.BlockSpec((1, LANES), lambda i, j: (0, 0))],
        out_specs=pl.BlockSpec((None, ROW_TILE, w), lambda i, j: (i, j, 0)),
        compiler_params=_cparams("parallel", "parallel"),
        name="gated_norm",
    )(o2, z, gain_row)


def _merge_kernel(h_ref, oa_ref, ob_ref, oc_ref, od_ref, wg_ref, wb_ref, wo_ref, x_ref, mod_ref, gains_ref,
                  wr_ref, br_ref, xn_ref, h2_ref, idx_ref, wgt_ref):
    d = x_ref.shape[1]
    h = h_ref[...]
    merged = None
    for n, o_ref in enumerate((oa_ref, ob_ref, oc_ref, od_ref)):
        gate = jax.nn.sigmoid(_dot(h, wg_ref[:, n * d:(n + 1) * d]))
        term = gate * _dot(o_ref[...], wb_ref[n])
        merged = term if merged is None else merged + term
    y = _dot(merged.astype(BF16), wo_ref[...])
    x_new = x_ref[...] + mod_ref[2:3, :] * (_rms(y) * gains_ref[1:2, :])
    xn_ref[...] = x_new
    h2 = _rms(x_new) * gains_ref[2:3, :] * (1.0 + mod_ref[4:5, :]) + mod_ref[3:4, :]
    for s in range(d // LANES):
        h2_ref[:, s, :] = h2[:, s * LANES:(s + 1) * LANES]
    logits = _dot_hi(h2, wr_ref[...]) + br_ref[...]
    ne = logits.shape[1]
    lane = lax.broadcasted_iota(jnp.int32, logits.shape, 1)
    vals = []
    for kk in range(TOP_K):
        m = jnp.max(logits, axis=-1, keepdims=True)
        sel = jnp.min(jnp.where(logits == m, lane, ne), axis=-1, keepdims=True)
        sel = jnp.minimum(sel, ne - 1)
        idx_ref[:, kk:kk + 1] = sel
        vals.append(m)
        logits = jnp.where(lane == sel, -jnp.inf, logits)
    es = [jnp.exp(vv - vals[0]) for vv in vals]
    tot = es[0] + es[1] + es[2] + es[3]
    for kk in range(TOP_K):
        wgt_ref[:, kk:kk + 1] = es[kk] / tot


def _merge(h, branch_outs, wg, wb, wo, x, modtab, gains, w_router, b_router, nct, tile0):
    b, t, d = x.shape
    nt = t // ROW_TILE - tile0
    ne = w_router.shape[1]
    rows = lambda wd: pl.BlockSpec((None, ROW_TILE, wd), lambda i, j: (i, j + tile0, 0))
    bw = branch_outs[0].shape[2]
    return pl.pallas_call(
        _merge_kernel,
        out_shape=[jax.ShapeDtypeStruct((b, t, d), F32),
                   jax.ShapeDtypeStruct((b, t, d // LANES, LANES), F32),
                   jax.ShapeDtypeStruct((b, t, TOP_K), jnp.int32),
                   jax.ShapeDtypeStruct((b, t, TOP_K), F32)],
        grid=(b, nt),
        in_specs=[rows(d), rows(bw), rows(bw), rows(bw), rows(bw),
                  pl.BlockSpec(wg.shape, lambda i, j: (0, 0), pipeline_mode=pl.Buffered(1)),
                  pl.BlockSpec(wb.shape, lambda i, j: (0, 0, 0), pipeline_mode=pl.Buffered(1)),
                  pl.BlockSpec(wo.shape, lambda i, j: (0, 0), pipeline_mode=pl.Buffered(1)),
                  rows(d),
                  pl.BlockSpec((None, None, N_MOD, d), lambda i, j: (i, (j + tile0 >= nct).astype(jnp.int32), 0, 0)),
                  pl.BlockSpec(gains.shape, lambda i, j: (0, 0)),
                  pl.BlockSpec(w_router.shape, lambda i, j: (0, 0)),
                  pl.BlockSpec((1, ne), lambda i, j: (0, 0))],
        out_specs=[rows(d),
                   pl.BlockSpec((None, ROW_TILE, d // LANES, LANES), lambda i, j: (i, j + tile0, 0, 0)),
                   rows(TOP_K), rows(TOP_K)],
        compiler_params=_cparams("parallel", "parallel"),
        name="merge",
    )(h, *branch_outs, wg, wb, wo, x, modtab, gains, w_router, b_router.reshape(1, ne))


def _rank_kernel(idx_ref, rank_ref, count_ref, carry_ref, *, ne):
    @pl.when(pl.program_id(0) == 0)
    def _():
        carry_ref[...] = jnp.zeros_like(carry_ref)

    idx = idx_ref[...]
    tm = idx.shape[0]
    lane = lax.broadcasted_iota(jnp.int32, (tm, ne), 1)
    onehots = [(lane == idx[:, kk:kk + 1]) for kk in range(TOP_K)]
    member = onehots[0] | onehots[1] | onehots[2] | onehots[3]
    i = lax.broadcasted_iota(jnp.int32, (tm, tm), 0)
    j = lax.broadcasted_iota(jnp.int32, (tm, tm), 1)
    before = _dot((j < i).astype(BF16), member.astype(BF16)) + carry_ref[...]
    for kk in range(TOP_K):
        rk = jnp.sum(jnp.where(onehots[kk], before, 0.0), axis=-1, keepdims=True)
        rank_ref[:, kk:kk + 1] = rk.astype(jnp.int32)
    carry_ref[...] = carry_ref[...] + jnp.sum(member.astype(F32), axis=0, keepdims=True)
    count_ref[...] = carry_ref[...].astype(jnp.int32)


def _moe_rank(top_idx, ne):
    n = top_idx.shape[0]
    return pl.pallas_call(
        functools.partial(_rank_kernel, ne=ne),
        out_shape=[jax.ShapeDtypeStruct((n, TOP_K), jnp.int32), jax.ShapeDtypeStruct((1, ne), jnp.int32)],
        grid=(n // ROW_TILE,),
        in_specs=[pl.BlockSpec((ROW_TILE, TOP_K), lambda i: (i, 0))],
        out_specs=[pl.BlockSpec((ROW_TILE, TOP_K), lambda i: (i, 0)), pl.BlockSpec((1, ne), lambda i: (0, 0))],
        scratch_shapes=[pltpu.VMEM((1, ne), F32)],
        compiler_params=_cparams("arbitrary"),
        name="moe_rank",
    )(top_idx)


def _row_copy(src, dst, sem):
    return pltpu.make_async_copy(src, dst, sem)


def _dispatch_kernel(dest_ref, h_hbm, buf_in, buf_hbm, sem):
    del buf_in
    base = pl.program_id(0) * ROW_TILE

    def issue(r, carry):
        for kk in range(TOP_K):
            _row_copy(h_hbm.at[base + r], buf_hbm.at[dest_ref[0, r * TOP_K + kk]], sem).start()
        return carry

    lax.fori_loop(0, ROW_TILE, issue, 0)
    _row_copy(buf_hbm.at[pl.ds(0, ROW_TILE * TOP_K)], buf_hbm.at[pl.ds(0, ROW_TILE * TOP_K)], sem).wait()


def _moe_dispatch(h2, dest, n_rows):
    n, s, l = h2.shape
    buf0 = jnp.zeros((n_rows, s, l), h2.dtype)
    dest2 = dest.reshape(n // ROW_TILE, 1, ROW_TILE * TOP_K)
    return pl.pallas_call(
        _dispatch_kernel,
        out_shape=jax.ShapeDtypeStruct((n_rows, s, l), h2.dtype),
        grid=(n // ROW_TILE,),
        in_specs=[pl.BlockSpec((None, 1, ROW_TILE * TOP_K), lambda i: (i, 0, 0), memory_space=pltpu.SMEM),
                  pl.BlockSpec(memory_space=pl.ANY),
                  pl.BlockSpec(memory_space=pl.ANY)],
        out_specs=pl.BlockSpec(memory_space=pl.ANY),
        scratch_shapes=[pltpu.SemaphoreType.DMA],
        input_output_aliases={2: 0},
        compiler_params=pltpu.CompilerParams(dimension_semantics=("arbitrary",), vmem_limit_bytes=VMEM_LIMIT,
                                             has_side_effects=True),
        name="moe_dispatch",
    )(dest2, h2, buf0)


def _ffn_kernel(te_ref, nu_ref, x_ref, wgu_ref, bgu_ref, wd_ref, bd_ref, y_ref):
    i = pl.program_id(0)
    ns = x_ref.shape[1]

    @pl.when(i < nu_ref[0])
    def _():
        x = jnp.concatenate([x_ref[:, s, :] for s in range(ns)], axis=-1).astype(BF16)
        gu = _dot(x, wgu_ref[...]) + bgu_ref[...]
        f = gu.shape[1] // 2
        gate = jnp.minimum(gu[:, :f], SWIGLU_LIMIT)
        up = jnp.clip(gu[:, f:], -SWIGLU_LIMIT, SWIGLU_LIMIT)
        act = (up + 1.0) * gate * jax.nn.sigmoid(SWIGLU_ALPHA * gate)
        y = _dot(act.astype(BF16), wd_ref[...]) + bd_ref[...]
        for s in range(ns):
            y_ref[:, s, :] = y[:, s * LANES:(s + 1) * LANES]

    @pl.when(i >= nu_ref[0])
    def _():
        y_ref[...] = jnp.zeros_like(y_ref)


def _moe_ffn(buf, tile_expert, n_used, wgu, bgu, wd, bd):
    n_rows, s, l = buf.shape
    ne, d, f2 = wgu.shape
    grid_spec = pltpu.PrefetchScalarGridSpec(
        num_scalar_prefetch=2,
        grid=(n_rows // MOE_TILE,),
        in_specs=[pl.BlockSpec((MOE_TILE, s, l), lambda i, te, nu: (i, 0, 0)),
                  pl.BlockSpec((None, d, f2), lambda i, te, nu: (te[i], 0, 0)),
                  pl.BlockSpec((None, 1, f2), lambda i, te, nu: (te[i], 0, 0)),
                  pl.BlockSpec((None, f2 // 2, d), lambda i, te, nu: (te[i], 0, 0)),
                  pl.BlockSpec((None, 1, d), lambda i, te, nu: (te[i], 0, 0))],
        out_specs=pl.BlockSpec((MOE_TILE, s, l), lambda i, te, nu: (i, 0, 0)),
    )
    return pl.pallas_call(
        _ffn_kernel,
        out_shape=jax.ShapeDtypeStruct((n_rows, s, l), F32),
        grid_spec=grid_spec,
        compiler_params=_cparams("arbitrary"),
        name="moe_ffn",
    )(tile_expert, n_used, buf, wgu, bgu.reshape(ne, 1, f2), wd, bd.reshape(ne, 1, d))


def _combine_kernel(dest_ref, wgt_ref, y_hbm, x_ref, mod_ref, gain_ref, o_ref, gat_ref, acc_ref, sem):
    def issue(r, carry):
        for kk in range(TOP_K):
            _row_copy(y_hbm.at[dest_ref[0, r * TOP_K + kk]], gat_ref.at[r * TOP_K + kk], sem).start()
        return carry

    lax.fori_loop(0, ROW_TILE, issue, 0)
    _row_copy(y_hbm.at[pl.ds(0, ROW_TILE * TOP_K)], gat_ref, sem).wait()

    def mix(r, carry):
        acc = None
        for kk in range(TOP_K):
            term = wgt_ref[0, r * TOP_K + kk] * gat_ref[r * TOP_K + kk]
            acc = term if acc is None else acc + term
        acc_ref[r] = acc
        return carry

    lax.fori_loop(0, ROW_TILE, mix, 0)
    f = jnp.concatenate([acc_ref[:, s, :] for s in range(acc_ref.shape[1])], axis=-1)
    o_ref[...] = x_ref[...] + mod_ref[5:6, :] * (_rms(f) * gain_ref[...])


def _moe_combine(ybuf, dest, wgt, x, modtab, gain_row, nct, tile0):
    b, t, d = x.shape
    tpb = t // ROW_TILE
    nt = tpb - tile0
    s, l = ybuf.shape[1:]
    dest2 = dest.reshape(b * tpb, 1, ROW_TILE * TOP_K)
    wgt2 = wgt.reshape(b * tpb, 1, ROW_TILE * TOP_K)
    smem_rows = lambda: pl.BlockSpec((None, 1, ROW_TILE * TOP_K), lambda i, j: (i * tpb + j + tile0, 0, 0),
                                     memory_space=pltpu.SMEM)
    return pl.pallas_call(
        _combine_kernel,
        out_shape=jax.ShapeDtypeStruct((b, t, d), F32),
        grid=(b, nt),
        in_specs=[smem_rows(), smem_rows(),
                  pl.BlockSpec(memory_space=pl.ANY),
                  pl.BlockSpec((None, ROW_TILE, d), lambda i, j: (i, j + tile0, 0)),
                  pl.BlockSpec((None, None, N_MOD, d), lambda i, j: (i, (j + tile0 >= nct).astype(jnp.int32), 0, 0)),
                  pl.BlockSpec((1, d), lambda i, j: (0, 0))],
        out_specs=pl.BlockSpec((None, ROW_TILE, d), lambda i, j: (i, j + tile0, 0)),
        scratch_shapes=[pltpu.VMEM((ROW_TILE * TOP_K, s, l), F32),
                        pltpu.VMEM((ROW_TILE, s, l), F32),
                        pltpu.SemaphoreType.DMA],
        compiler_params=_cparams("arbitrary", "arbitrary"),
        name="moe_combine",
    )(dest2, wgt2, ybuf, x, modtab, gain_row)


def _deinterleave(width):
    idx = np.arange(width).reshape(-1, HEAD_DIM // 2, 2)
    return np.concatenate([idx[:, :, 0], idx[:, :, 1]], axis=1).reshape(-1)


def _rope_tables(ang, ctx_len):
    cos = jnp.cos(ang)
    sin = jnp.sin(ang)
    cos64 = jnp.concatenate([cos, cos], axis=-1)
    sin64 = jnp.concatenate([-sin, sin], axis=-1)
    reps = LANES // HEAD_DIM
    cos_t = jnp.concatenate([jnp.ones((ctx_len, HEAD_DIM), F32), cos64], axis=0)
    sin_t = jnp.concatenate([jnp.zeros((ctx_len, HEAD_DIM), F32), sin64], axis=0)
    return jnp.tile(cos_t, (1, reps)), jnp.tile(sin_t, (1, reps))


def _axial_angles(n_tok):
    rows = n_tok // GRID_W
    row = jnp.repeat(jnp.arange(rows), GRID_W).astype(F32)
    col = jnp.tile(jnp.arange(GRID_W), rows).astype(F32)
    half = HEAD_DIM // 2
    inv = ROPE_THETA ** (-jnp.arange(0, half, 2, dtype=F32) / half)
    return jnp.concatenate([row[:, None] * inv, col[:, None] * inv], axis=-1)


def _line_angles(n_tok):
    pos = jnp.arange(n_tok, dtype=F32)
    inv = ROPE_THETA ** (-jnp.arange(0, RET_QK_DIM, 2, dtype=F32) / RET_QK_DIM)
    return pos[:, None] * inv


def _layer(x, modtab, p, layer_idx, ctx_len, last):
    b, t, d = x.shape
    nct = ctx_len // ROW_TILE
    del last
    tile0 = 0
    bwid = d // 2
    sizes = (bwid, GQA_KV_HEADS * HEAD_DIM, GQA_KV_HEADS * HEAD_DIM,
             bwid, bwid, bwid,
             RET_HEADS * RET_QK_DIM, RET_HEADS * RET_QK_DIM, bwid, bwid,
             3 * bwid, bwid, 2 * DN_HEADS, 2 * DN_HEADS,
             N_BRANCHES * d)
    cuts = np.concatenate([[0], np.cumsum(sizes)])
    w_in = p['w_in']
    col = lambda a, e: w_in[:, cuts[a]:cuts[e]]
    gains = p['norm_gain']
    ones_row = jnp.ones((1, LANES), F32)

    h = _modulate(x, modtab, gains[0:1], nct)

    seq = t - ctx_len
    cos_ax, sin_ax = _rope_tables(_axial_angles(seq), ctx_len)
    cos_ln, sin_ln = _rope_tables(_line_angles(seq), ctx_len)

    perm_q = _deinterleave(sizes[0])
    perm_k = _deinterleave(sizes[1])
    wa = jnp.concatenate([col(0, 1)[:, perm_q], col(1, 2)[:, perm_k], col(2, 3)], axis=1).astype(BF16)
    perm64 = _deinterleave(HEAD_DIM)
    qk_gain = p['gqa_qk_gain'].astype(F32)
    gain_a = jnp.concatenate([jnp.tile(qk_gain[0][perm64], sizes[0] // HEAD_DIM),
                              jnp.tile(qk_gain[1][perm64], sizes[1] // HEAD_DIM)]).reshape(1, -1)
    qa, ka, va = _inproj(h, wa, cos_ax, sin_ax, gain_a,
                         [(sizes[0], BF16), (sizes[1], BF16), (sizes[2], BF16)], sizes[0] + sizes[1], True)
    oa = _gqa_attention(qa, ka, va, nct, ctx_len, tile0)

    perm_b = _deinterleave(bwid)
    wb_in = jnp.concatenate([col(3, 4)[:, perm_b], col(4, 5)[:, perm_b], col(5, 6)], axis=1).astype(BF16)
    qb, kb, vb = _inproj(h, wb_in, cos_ax, sin_ax, ones_row,
                         [(bwid, BF16)] * 3, 2 * bwid, False)
    lam_init = 0.8 - 0.6 * math.exp(-0.3 * layer_idx)
    ob = _diff_attention(qb, kb, vb, p['diff_lambda'].astype(F32), p['diff_norm'].astype(F32).reshape(1, -1),
                         nct, ctx_len, tile0, lam_init)

    perm_c = _deinterleave(sizes[6])
    wc = jnp.concatenate([col(6, 7)[:, perm_c], col(7, 8)[:, perm_c] * (RET_QK_DIM ** -0.5), col(8, 10)],
                         axis=1).astype(BF16)
    qc, kc, vc, gc = _inproj(h, wc, cos_ln, sin_ln, ones_row,
                             [(sizes[6], BF16), (sizes[7], BF16), (bwid, BF16), (bwid, BF16)],
                             sizes[6] + sizes[7], False)
    log_gamma = jax.nn.log_sigmoid(p['ret_decay_logit'].astype(F32))
    oc2 = _retention(qc, kc, vc, log_gamma, ctx_len)
    oc = _gated_norm(oc2, gc, ones_row)

    wd_in = col(10, 12).astype(BF16)
    w_ba = col(12, 14).astype(BF16)
    w_ba = jnp.pad(w_ba, ((0, 0), (0, LANES - w_ba.shape[1])))
    wd_all = jnp.concatenate([wd_in, w_ba], axis=1)
    qkv_d, z_d, ba_d = _inproj(h, wd_all, cos_ln, sin_ln, ones_row,
                               [(3 * bwid, F32), (bwid, BF16), (LANES, F32)], 0, False)
    nb = 2 * DN_HEADS
    rate_row = jnp.exp(p['dn_a_log'].astype(F32)).reshape(1, nb)
    bias_row = p['dn_dt_bias'].astype(F32).reshape(1, nb)
    qd, kd, vd, g_d, beta_d = _dn_prep(qkv_d, ba_d[:, :, :2 * nb], p['dn_conv_w'].astype(F32), rate_row, bias_row, nct)
    od2 = _deltanet(qd, kd, vd, g_d, beta_d, ctx_len)
    od = _gated_norm(od2, z_d, p['dn_norm'].astype(F32).reshape(1, -1))

    wg = col(14, 15).astype(BF16)
    x_new, h2, top_idx, top_w = _merge(h, (oa, ob, oc, od), wg, p['w_branch'].astype(BF16),
                                       p['w_out'].astype(BF16), x, modtab, gains,
                                       p['w_router'].astype(F32), p['b_router'].astype(F32), nct, tile0)

    ne = p['w_router'].shape[1]
    n = b * t
    top_idx = top_idx.reshape(n, TOP_K)
    rank, counts = _moe_rank(top_idx, ne)
    counts = counts.reshape(ne)
    padded = (counts + MOE_TILE - 1) // MOE_TILE * MOE_TILE
    pad_ends = jnp.cumsum(padded)
    pad_starts = pad_ends - padded
    dest = pad_starts[top_idx] + rank
    n_tiles = (n * TOP_K) // MOE_TILE + ne
    tile_expert = jnp.minimum(jnp.searchsorted(pad_ends, jnp.arange(n_tiles) * MOE_TILE, side='right'),
                              ne - 1).astype(jnp.int32)
    n_used = (pad_ends[-1] // MOE_TILE).astype(jnp.int32).reshape(1)
    buf = _moe_dispatch(h2.reshape(n, d // LANES, LANES), dest, n_tiles * MOE_TILE)
    ybuf = _moe_ffn(buf, tile_expert, n_used, p['w_gate_up'].astype(BF16), p['b_gate_up'].astype(F32),
                    p['w_down'].astype(BF16), p['b_down'].astype(F32))
    return _moe_combine(ybuf, dest, top_w.reshape(n, TOP_K), x_new, modtab, gains[3:4], nct, tile0)


def kernel(x, c, ctx, c_ctx, w_mod, b_mod, norm_gain, w_in, gqa_qk_gain, diff_lambda, diff_norm, ret_decay_logit,
           dn_conv_w, dn_a_log, dn_dt_bias, dn_norm, w_branch, w_out, w_router, b_router, w_gate_up, b_gate_up,
           w_down, b_down):
    b, seq, d = x.shape
    ctx_len = ctx.shape[1]
    depth = w_mod.shape[0]
    assert ctx_len % ROW_TILE == 0 and seq % ROW_TILE == 0 and seq % GRID_W == 0 and d % LANES == 0
    xa = jnp.concatenate([ctx, x], axis=1).astype(F32)
    c_all = jnp.concatenate([c, c_ctx[None, :]], axis=0).astype(F32)
    for l in range(depth):
        p = {
            'norm_gain': norm_gain[l].astype(F32), 'w_in': w_in[l], 'gqa_qk_gain': gqa_qk_gain[l],
            'diff_lambda': diff_lambda[l], 'diff_norm': diff_norm[l], 'ret_decay_logit': ret_decay_logit[l],
            'dn_conv_w': dn_conv_w[l], 'dn_a_log': dn_a_log[l], 'dn_dt_bias': dn_dt_bias[l], 'dn_norm': dn_norm[l],
            'w_branch': w_branch[l], 'w_out': w_out[l], 'w_router': w_router[l], 'b_router': b_router[l],
            'w_gate_up': w_gate_up[l], 'b_gate_up': b_gate_up[l], 'w_down': w_down[l], 'b_down': b_down[l],
        }
        mod = _mod_table(c_all, w_mod[l].astype(F32), b_mod[l].astype(F32))
        mod = mod.reshape(b + 1, N_MOD, d)
        modtab = jnp.stack([jnp.broadcast_to(mod[b], (b, N_MOD, d)), mod[:b]], axis=1)
        xa = _layer(xa, modtab, p, l, ctx_len, l == depth - 1)
    return xa[:, ctx_len:, :]
```

```python
import functools
import math

import numpy as np
import jax
import jax.numpy as jnp
from jax import lax
from jax.experimental import pallas as pl
from jax.experimental.pallas import tpu as pltpu

F32 = jnp.float32
BF16 = jnp.bfloat16

GRID_W = 64
RMS_EPS = 1e-6
ROPE_THETA = 10000.0
HEAD_DIM = 64
GQA_KV_HEADS = 2
GQA_GROUP = 4
DIFF_HEADS = 4
RET_HEADS = 4
RET_QK_DIM = 64
RET_V_DIM = 128
DN_HEADS = 4
DN_HEAD_DIM = 128
DN_CONV = 5
DN_CHUNK = 64
TOP_K = 4
SWIGLU_LIMIT = 7.0
SWIGLU_ALPHA = 1.702
N_MOD = 6
N_BRANCHES = 4

LANES = 128
SUBLANES = 8
ROW_TILE = 256
RET_CHUNK = 128
DN_BATCH_CHUNKS = 4
MOE_TILE = 512
MOE_SUB = 256
VMEM_LIMIT = 56 * 1024 * 1024


def _cparams(*sem):
    return pltpu.CompilerParams(dimension_semantics=sem, vmem_limit_bytes=VMEM_LIMIT)


def _const_spec(shape):
    nd = len(shape)
    return pl.BlockSpec(shape, lambda *_: (0,) * nd)


def _rms(x):
    return x * lax.rsqrt(jnp.mean(x * x, axis=-1, keepdims=True) + RMS_EPS)


def _silu(x):
    return x * jax.nn.sigmoid(x)


def _dot(a, b):
    return jnp.dot(a, b, preferred_element_type=F32)


def _dot_nt(a, b):
    return lax.dot_general(a, b, (((1,), (1,)), ((), ())), preferred_element_type=F32)


def _dot_tn(a, b):
    return lax.dot_general(a, b, (((0,), (0,)), ((), ())), preferred_element_type=F32)


def _dot_hi(a, b):
    return jnp.dot(a, b, preferred_element_type=F32, precision=lax.Precision.HIGHEST)


def _mod_kernel(c_ref, w_ref, b_ref, o_ref):
    o_ref[...] = _dot(_silu(c_ref[...]), w_ref[...]) + b_ref[...]


def _mod_table(c_all, w_mod, b_mod):
    m, d = c_all.shape
    n = w_mod.shape[1]
    tn = d
    return pl.pallas_call(
        _mod_kernel,
        out_shape=jax.ShapeDtypeStruct((m, n), F32),
        grid=(n // tn,),
        in_specs=[pl.BlockSpec((m, d), lambda j: (0, 0)),
                  pl.BlockSpec((d, tn), lambda j: (0, j)),
                  pl.BlockSpec((1, tn), lambda j: (0, j))],
        out_specs=pl.BlockSpec((m, tn), lambda j: (0, j)),
        compiler_params=_cparams("parallel"),
        name="mod_table",
    )(c_all, w_mod, b_mod.reshape(1, n))


def _modulate_kernel(x_ref, mod_ref, gain_ref, h_ref):
    xn = _rms(x_ref[...]) * gain_ref[...]
    h = xn * (1.0 + mod_ref[1:2, :]) + mod_ref[0:1, :]
    h_ref[...] = h.astype(h_ref.dtype)


def _modulate(x, modtab, gain_row, nct):
    b, t, d = x.shape
    return pl.pallas_call(
        _modulate_kernel,
        out_shape=jax.ShapeDtypeStruct((b, t, d), BF16),
        grid=(b, t // ROW_TILE),
        in_specs=[pl.BlockSpec((None, ROW_TILE, d), lambda i, j: (i, j, 0)),
                  pl.BlockSpec((None, None, N_MOD, d), lambda i, j: (i, (j >= nct).astype(jnp.int32), 0, 0)),
                  pl.BlockSpec((1, d), lambda i, j: (0, 0))],
        out_specs=pl.BlockSpec((None, ROW_TILE, d), lambda i, j: (i, j, 0)),
        compiler_params=_cparams("parallel", "parallel"),
        name="modulate",
    )(x, modtab, gain_row)


def _swap_halves(y):
    lane = lax.broadcasted_iota(jnp.int32, y.shape, 1)
    first = (lane % HEAD_DIM) < (HEAD_DIM // 2)
    return jnp.where(first, pltpu.roll(y, LANES - HEAD_DIM // 2, 1), pltpu.roll(y, HEAD_DIM // 2, 1))


def _inproj_kernel(h_ref, w_ref, cos_ref, sin_ref, gain_ref, seg_ref, *out_refs, widths, n_rope, do_norm):
    u = _dot(h_ref[...], w_ref[...])
    cos = cos_ref[...]
    sin = sin_ref[...]
    col = 0
    for o_ref, wd in zip(out_refs, widths):
        for j in range(0, wd, LANES):
            wj = min(LANES, wd - j)
            y = u[:, col + j: col + j + wj]
            if col + j < n_rope:
                if do_norm:
                    ms = _dot((y * y).astype(BF16), seg_ref[...])
                    y = y * lax.rsqrt(ms + RMS_EPS) * gain_ref[:, col + j: col + j + wj]
                y = y * cos + _swap_halves(y) * sin
            o_ref[:, j: j + wj] = y.astype(o_ref.dtype)
        col += wd


def _inproj(h, w, cos, sin, gain_row, outs, n_rope, do_norm):
    b, t, d = h.shape
    c = w.shape[1]
    widths = tuple(o[0] for o in outs)
    assert sum(widths) == c
    seg = np.kron(np.eye(LANES // HEAD_DIM), np.full((HEAD_DIM, HEAD_DIM), 1.0 / HEAD_DIM))
    seg = jnp.asarray(seg, BF16)
    kern = functools.partial(_inproj_kernel, widths=widths, n_rope=n_rope, do_norm=do_norm)
    return pl.pallas_call(
        kern,
        out_shape=[jax.ShapeDtypeStruct((b, t, wd), dt) for wd, dt in outs],
        grid=(b, t // ROW_TILE),
        in_specs=[pl.BlockSpec((None, ROW_TILE, d), lambda i, j: (i, j, 0)),
                  pl.BlockSpec((d, c), lambda i, j: (0, 0)),
                  pl.BlockSpec((ROW_TILE, LANES), lambda i, j: (j, 0)),
                  pl.BlockSpec((ROW_TILE, LANES), lambda i, j: (j, 0)),
                  pl.BlockSpec(gain_row.shape, lambda i, j: (0, 0)),
                  pl.BlockSpec((LANES, LANES), lambda i, j: (0, 0))],
        out_specs=[pl.BlockSpec((None, ROW_TILE, wd), lambda i, j: (i, j, 0)) for wd, _ in outs],
        compiler_params=_cparams("parallel", "parallel"),
        name="inproj",
    )(h, w, cos, sin, gain_row, seg)


def _softmax_pv(s, v):
    m = jnp.max(s, axis=-1, keepdims=True)
    p = jnp.exp(s - m)
    l = jnp.sum(p, axis=-1, keepdims=True)
    return _dot(p.astype(BF16), v) / l


def _gqa_kernel(q_ref, k_ref, v_ref, o_ref, *, nct, ctx_len, tile0):
    t = pl.program_id(1) + tile0
    scale = HEAD_DIM ** -0.5

    def run(nk):
        for kv in range(GQA_KV_HEADS):
            kh = k_ref[0:nk, kv * HEAD_DIM:(kv + 1) * HEAD_DIM]
            vh = v_ref[0:nk, kv * HEAD_DIM:(kv + 1) * HEAD_DIM]
            for g in range(GQA_GROUP):
                c0 = (kv * GQA_GROUP + g) * HEAD_DIM
                s = _dot_nt(q_ref[:, c0:c0 + HEAD_DIM], kh) * scale
                o_ref[:, c0:c0 + HEAD_DIM] = _softmax_pv(s, vh).astype(o_ref.dtype)

    if tile0 < nct:
        @pl.when(t < nct)
        def _():
            run(ctx_len)

    @pl.when(t >= nct)
    def _():
        run(k_ref.shape[0])


def _gqa_attention(q, k, v, nct, ctx_len, tile0):
    b, t, cq = q.shape
    nt = t // ROW_TILE - tile0
    kern = functools.partial(_gqa_kernel, nct=nct, ctx_len=ctx_len, tile0=tile0)
    return pl.pallas_call(
        kern,
        out_shape=jax.ShapeDtypeStruct((b, t, cq), BF16),
        grid=(b, nt),
        in_specs=[pl.BlockSpec((None, ROW_TILE, cq), lambda i, j: (i, j + tile0, 0)),
                  pl.BlockSpec((None, t, k.shape[2]), lambda i, j: (i, 0, 0)),
                  pl.BlockSpec((None, t, v.shape[2]), lambda i, j: (i, 0, 0))],
        out_specs=pl.BlockSpec((None, ROW_TILE, cq), lambda i, j: (i, j + tile0, 0)),
        compiler_params=_cparams("parallel", "parallel"),
        name="gqa_attention",
    )(q, k, v)


def _diff_kernel(q_ref, k_ref, v_ref, lam_ref, gain_ref, o_ref, *, nct, ctx_len, tile0, lam_init):
    t = pl.program_id(1) + tile0
    scale = HEAD_DIM ** -0.5
    lp = lam_ref[...]
    lam = (jnp.exp(jnp.sum(lp[0:1, :] * lp[1:2, :], axis=-1, keepdims=True))
           - jnp.exp(jnp.sum(lp[2:3, :] * lp[3:4, :], axis=-1, keepdims=True)) + lam_init)
    dv = 2 * HEAD_DIM

    def run(nk):
        for hd in range(DIFF_HEADS):
            c0 = hd * dv
            vh = v_ref[0:nk, c0:c0 + dv]
            s1 = _dot_nt(q_ref[:, c0:c0 + HEAD_DIM], k_ref[0:nk, c0:c0 + HEAD_DIM]) * scale
            o1 = _softmax_pv(s1, vh)
            s2 = _dot_nt(q_ref[:, c0 + HEAD_DIM:c0 + dv], k_ref[0:nk, c0 + HEAD_DIM:c0 + dv]) * scale
            o2 = _softmax_pv(s2, vh)
            o = _rms(o1 - lam * o2) * gain_ref[...] * (1.0 - lam_init)
            o_ref[:, c0:c0 + dv] = o.astype(o_ref.dtype)

    if tile0 < nct:
        @pl.when(t < nct)
        def _():
            run(ctx_len)

    @pl.when(t >= nct)
    def _():
        run(k_ref.shape[0])


def _diff_attention(q, k, v, lam_params, gain_row, nct, ctx_len, tile0, lam_init):
    b, t, cq = q.shape
    nt = t // ROW_TILE - tile0
    kern = functools.partial(_diff_kernel, nct=nct, ctx_len=ctx_len, tile0=tile0, lam_init=lam_init)
    return pl.pallas_call(
        kern,
        out_shape=jax.ShapeDtypeStruct((b, t, cq), BF16),
        grid=(b, nt),
        in_specs=[pl.BlockSpec((None, ROW_TILE, cq), lambda i, j: (i, j + tile0, 0)),
                  pl.BlockSpec((None, t, cq), lambda i, j: (i, 0, 0)),
                  pl.BlockSpec((None, t, cq), lambda i, j: (i, 0, 0)),
                  pl.BlockSpec(lam_params.shape, lambda i, j: (0, 0)),
                  pl.BlockSpec(gain_row.shape, lambda i, j: (0, 0))],
        out_specs=pl.BlockSpec((None, ROW_TILE, cq), lambda i, j: (i, j + tile0, 0)),
        compiler_params=_cparams("parallel", "parallel"),
        name="diff_attention",
    )(q, k, v, lam_params, gain_row)


def _scan_chunk(d, s, n_ctx_chunks, n_chunks):
    rev = jnp.where(s < n_ctx_chunks, n_ctx_chunks - 1 - s, n_chunks + n_ctx_chunks - 1 - s)
    return jnp.where(d == 0, s, rev)


def _ret_kernel(lg_ref, qf_ref, kf_ref, vf_ref, qb_ref, kb_ref, vb_ref, of_ref, ob_ref, state_ref):
    c = RET_CHUNK
    nh = RET_HEADS

    @pl.when(pl.program_id(1) == 0)
    def _():
        state_ref[...] = jnp.zeros_like(state_ref)

    i = lax.broadcasted_iota(jnp.int32, (c, c), 0)
    j = lax.broadcasted_iota(jnp.int32, (c, c), 1)
    r = lax.broadcasted_iota(jnp.int32, (c, 1), 0)
    lg = jnp.stack([jnp.full((1, 1), lg_ref[d, hd], F32) for d in range(2) for hd in range(nh)])
    rel = jnp.stack([i - j] * nh + [j - i] * nh)
    intra = jnp.where(rel >= 0, jnp.exp(lg * jnp.maximum(rel, 0).astype(F32)), 0.0)
    pos_q = jnp.stack([r + 1] * nh + [c - r] * nh).astype(F32)
    pos_k = jnp.stack([c - 1 - r] * nh + [r] * nh).astype(F32)

    def stacked(refs, width):
        return jnp.stack([ref[:, hd * width:(hd + 1) * width] for ref in refs for hd in range(nh)])

    q = stacked((qf_ref, qb_ref), RET_QK_DIM)
    k = stacked((kf_ref, kb_ref), RET_QK_DIM)
    v = stacked((vf_ref, vb_ref), RET_V_DIM)
    st = state_ref[...]
    scores = _bdot_nt(q, k) * intra
    qd = (q.astype(F32) * jnp.exp(lg * pos_q)).astype(BF16)
    o = _bdot(scores.astype(BF16), v) + _bdot(qd, st.astype(BF16))
    kd = (k.astype(F32) * jnp.exp(lg * pos_k)).astype(BF16)
    state_ref[...] = st * jnp.exp(lg * c) + _bdot_tn(kd, v)
    for n in range(2 * nh):
        d, hd = divmod(n, nh)
        (of_ref, ob_ref)[d][:, hd * RET_V_DIM:(hd + 1) * RET_V_DIM] = o[n]


def _retention(q, k, v, log_gamma, ctx_len):
    b, t, _ = q.shape
    n = t // RET_CHUNK
    nc = ctx_len // RET_CHUNK

    def specs(d):
        rows = lambda a: pl.BlockSpec((None, RET_CHUNK, a.shape[2]), lambda i, s: (i, _scan_chunk(d, s, nc, n), 0))
        return [rows(q), rows(k), rows(v)]

    return pl.pallas_call(
        _ret_kernel,
        out_shape=[jax.ShapeDtypeStruct((b, t, v.shape[2]), F32)] * 2,
        grid=(b, n),
        in_specs=[pl.BlockSpec(memory_space=pltpu.SMEM)] + specs(0) + specs(1),
        out_specs=[specs(0)[2], specs(1)[2]],
        scratch_shapes=[pltpu.VMEM((2 * RET_HEADS, RET_QK_DIM, RET_V_DIM), F32)],
        compiler_params=_cparams("parallel", "arbitrary"),
        name="retention_scan",
    )(log_gamma, q, k, v, q, k, v)


def _dn_prep_kernel(prev_ref, cur_ref, next_ref, cw_ref, ba_ref, rate_ref, bias_ref,
                    q_ref, k_ref, v_ref, g_ref, beta_ref, *, nct, n_tiles):
    t = pl.program_id(1)
    pad = DN_CONV // 2
    first = (t == 0) | (t == nct)
    last = (t == nct - 1) | (t == n_tiles - 1)
    prev = jnp.where(first, 0.0, prev_ref[...])
    nxt = jnp.where(last, 0.0, next_ref[...])
    ext = jnp.concatenate([prev, cur_ref[...], nxt], axis=0)
    rows = ext.shape[0]
    acc = None
    for tap in range(DN_CONV):
        sh = (pad - tap) % rows
        x = ext if sh == 0 else pltpu.roll(ext, sh, 0)
        term = x[SUBLANES:SUBLANES + ROW_TILE, :] * cw_ref[tap:tap + 1, :]
        acc = term if acc is None else acc + term
    y = _silu(acc)
    hw = DN_HEADS * DN_HEAD_DIM
    for hd in range(DN_HEADS):
        sl = slice(hd * DN_HEAD_DIM, (hd + 1) * DN_HEAD_DIM)
        qh = y[:, sl]
        kh = y[:, hw + hd * DN_HEAD_DIM: hw + (hd + 1) * DN_HEAD_DIM]
        q_ref[:, sl] = qh * lax.rsqrt(jnp.sum(qh * qh, axis=-1, keepdims=True) + RMS_EPS) * (DN_HEAD_DIM ** -0.5)
        k_ref[:, sl] = kh * lax.rsqrt(jnp.sum(kh * kh, axis=-1, keepdims=True) + RMS_EPS)
    v_ref[...] = y[:, 2 * hw:3 * hw]
    nb = 2 * DN_HEADS
    ba = ba_ref[...]
    beta_ref[...] = jax.nn.sigmoid(ba[:, 0:nb])
    a = ba[:, nb:2 * nb] + bias_ref[...]
    softplus = jnp.maximum(a, 0.0) + jnp.log(1.0 + jnp.exp(-jnp.abs(a)))
    g_ref[...] = -rate_ref[...] * softplus


def _dn_prep(qkv, ba, conv_w, rate_row, bias_row, nct):
    b, t, c = qkv.shape
    n_tiles = t // ROW_TILE
    hb = ROW_TILE // SUBLANES
    last_blk = t // SUBLANES - 1
    hw = DN_HEADS * DN_HEAD_DIM
    nb = 2 * DN_HEADS
    kern = functools.partial(_dn_prep_kernel, nct=nct, n_tiles=n_tiles)
    row_spec = lambda wd: pl.BlockSpec((None, ROW_TILE, wd), lambda i, j: (i, j, 0))
    return pl.pallas_call(
        kern,
        out_shape=[jax.ShapeDtypeStruct((b, t, hw), F32)] * 3 + [jax.ShapeDtypeStruct((b, t, nb), F32)] * 2,
        grid=(b, n_tiles),
        in_specs=[pl.BlockSpec((None, SUBLANES, c), lambda i, j: (i, jnp.maximum(j * hb - 1, 0), 0)),
                  row_spec(c),
                  pl.BlockSpec((None, SUBLANES, c), lambda i, j: (i, jnp.minimum((j + 1) * hb, last_blk), 0)),
                  pl.BlockSpec(conv_w.shape, lambda i, j: (0, 0)),
                  row_spec(2 * nb),
                  pl.BlockSpec((1, nb), lambda i, j: (0, 0)),
                  pl.BlockSpec((1, nb), lambda i, j: (0, 0))],
        out_specs=[row_spec(hw)] * 3 + [row_spec(nb)] * 2,
        compiler_params=_cparams("parallel", "parallel"),
        name="deltanet_prep",
    )(qkv, qkv, qkv, conv_w, ba, rate_row, bias_row)


def _split2(x):
    hi = x.astype(BF16)
    lo = (x - hi.astype(F32)).astype(BF16)
    return hi, lo


def _bdot(a, b):
    return lax.dot_general(a, b, (((2,), (1,)), ((0,), (0,))), preferred_element_type=F32)


def _bdot_nt(a, b):
    return lax.dot_general(a, b, (((2,), (2,)), ((0,), (0,))), preferred_element_type=F32)


def _bdot_tn(a, b):
    return lax.dot_general(a, b, (((1,), (1,)), ((0,), (0,))), preferred_element_type=F32)


def _bdot_bf(a, b):
    return _bdot(a.astype(BF16), b.astype(BF16))


def _bdot_01(a01_bf16, x):
    xh, xl = _split2(x)
    return _bdot(a01_bf16, xh) + _bdot(a01_bf16, xl)


def _dn_local_kernel(q_ref, k_ref, v_ref, g_ref, beta_ref, u_ref, w_ref, qg_ref, kd_ref, attn_ref, el_ref):
    c = DN_CHUNK
    dh = DN_HEAD_DIM
    i = lax.broadcasted_iota(jnp.int32, (c, c), 0)
    j = lax.broadcasted_iota(jnp.int32, (c, c), 1)
    eye = (i == j).astype(F32)

    nh = DN_HEADS
    nu = 2 * nh * DN_BATCH_CHUNKS
    rel = jnp.stack(([i - j] * nh + [j - i] * nh) * DN_BATCH_CHUNKS)
    incl = rel >= 0
    strict = rel > 0
    tri = incl.astype(BF16)

    def chunk(ci, carry):
        r0 = pl.multiple_of(ci * (c * DN_BATCH_CHUNKS), c * DN_BATCH_CHUNKS)
        rows_of = [pl.ds(r0 + m * c, c) for m in range(DN_BATCH_CHUNKS)]
        g_all = [g_ref[rows, :] for rows in rows_of]
        b_all = [beta_ref[rows, :] for rows in rows_of]
        g = jnp.stack([ga[:, n:n + 1] for ga in g_all for n in range(2 * nh)])
        beta = jnp.stack([ba[:, n:n + 1] for ba in b_all for n in range(2 * nh)])
        heads = lambda ref: jnp.stack([ref[rows, hd * dh:(hd + 1) * dh]
                                       for rows in rows_of for _ in range(2) for hd in range(nh)])
        q = heads(q_ref)
        k = heads(k_ref)
        v = heads(v_ref)
        gc = _bdot_01(tri, jnp.broadcast_to(g, (nu, c, dh)))
        dmat = _bdot_01(tri, jnp.where(strict, jnp.broadcast_to(g, (nu, c, c)), 0.0))
        decay = jnp.where(incl, jnp.exp(dmat), 0.0)
        kb = k * beta
        kbf = k.astype(BF16)
        a = jnp.where(strict, -(_bdot_nt(kb.astype(BF16), kbf) * decay), 0.0)
        tinv = eye + a
        p = a
        for _ in range(5):
            pb = p.astype(BF16)
            p = _bdot(pb, pb)
            tinv = tinv + _bdot_bf(tinv, p)
        egc = jnp.exp(gc)
        tb = tinv.astype(BF16)
        u = _bdot(tb, (v * beta).astype(BF16))
        w = _bdot(tb, (kb * egc).astype(BF16))
        attn = jnp.where(incl, _bdot_nt(q.astype(BF16), kbf) * decay, 0.0)
        ends = [c - 1 if (n // nh) % 2 == 0 else 0 for n in range(nu)]
        last = jnp.concatenate([gc[n:n + 1, e:e + 1, :] for n, e in enumerate(ends)], axis=0)
        kd = k * jnp.exp(last - gc)
        qg = q * egc
        el = jnp.exp(last)
        for n in range(nu):
            m, rest = divmod(n, 2 * nh)
            d, hd = divmod(rest, nh)
            rows = rows_of[m]
            sl = slice(hd * dh, (hd + 1) * dh)
            u_ref[d, rows, sl] = u[n]
            w_ref[d, rows, sl] = w[n].astype(w_ref.dtype)
            attn_ref[d, rows, hd * c:(hd + 1) * c] = attn[n].astype(attn_ref.dtype)
            kd_ref[d, rows, sl] = kd[n].astype(kd_ref.dtype)
            qg_ref[d, rows, sl] = qg[n].astype(qg_ref.dtype)
            el_ref[d, ci * DN_BATCH_CHUNKS + m, :, sl] = el[n]
        return carry

    lax.fori_loop(0, ROW_TILE // (c * DN_BATCH_CHUNKS), chunk, 0)


def _dn_local(q, k, v, g, beta):
    b, t, hw = q.shape
    cpt = ROW_TILE // DN_CHUNK
    rows = lambda wd: pl.BlockSpec((None, ROW_TILE, wd), lambda i, j: (i, j, 0))
    rows2 = lambda wd: pl.BlockSpec((2, None, ROW_TILE, wd), lambda i, j: (0, i, j, 0))
    aw = DN_HEADS * DN_CHUNK
    return pl.pallas_call(
        _dn_local_kernel,
        out_shape=[jax.ShapeDtypeStruct((2, b, t, hw), F32),
                   jax.ShapeDtypeStruct((2, b, t, hw), BF16),
                   jax.ShapeDtypeStruct((2, b, t, hw), BF16),
                   jax.ShapeDtypeStruct((2, b, t, hw), BF16),
                   jax.ShapeDtypeStruct((2, b, t, aw), BF16),
                   jax.ShapeDtypeStruct((2, b, t // DN_CHUNK, 1, hw), F32)],
        grid=(b, t // ROW_TILE),
        in_specs=[rows(hw), rows(hw), rows(hw), rows(g.shape[2]), rows(g.shape[2])],
        out_specs=[rows2(hw), rows2(hw), rows2(hw), rows2(hw), rows2(aw),
                   pl.BlockSpec((2, None, cpt, 1, hw), lambda i, j: (0, i, j, 0, 0))],
        compiler_params=_cparams("parallel", "parallel"),
        name="deltanet_local",
    )(q, k, v, g, beta)


def _dn_scan_kernel(*refs):
    ins = refs[:12]
    of_ref, ob_ref, state_ref = refs[12:]
    dh = DN_HEAD_DIM
    c = DN_CHUNK

    @pl.when(pl.program_id(1) == 0)
    def _():
        state_ref[...] = jnp.zeros_like(state_ref)

    nh = DN_HEADS

    def stacked(k, width):
        return jnp.stack([ins[6 * d + k][:, hd * width:(hd + 1) * width] for d in range(2) for hd in range(nh)])

    st = state_ref[...]
    stb = st.astype(BF16)
    v_new = stacked(0, dh) - _bdot(stacked(1, dh), stb)
    vnb = v_new.astype(BF16)
    o = _bdot(stacked(2, dh), stb) + _bdot(stacked(4, c), vnb)
    state_ref[...] = st * stacked(5, dh) + _bdot_tn(stacked(3, dh), vnb)
    for n in range(2 * nh):
        d, hd = divmod(n, nh)
        (of_ref, ob_ref)[d][:, hd * dh:(hd + 1) * dh] = o[n]


def _dn_scan(u, w, qg, kd, attn, el, ctx_len):
    _, b, t, hw = u.shape
    n = t // DN_CHUNK
    nc = ctx_len // DN_CHUNK
    aw = attn.shape[3]

    def specs(d):
        rows = lambda wd: pl.BlockSpec((None, None, DN_CHUNK, wd), lambda i, s: (d, i, _scan_chunk(d, s, nc, n), 0))
        return [rows(hw), rows(hw), rows(hw), rows(hw), rows(aw),
                pl.BlockSpec((None, None, None, 1, hw), lambda i, s: (d, i, _scan_chunk(d, s, nc, n), 0, 0))]

    out = lambda d: pl.BlockSpec((None, DN_CHUNK, hw), lambda i, s: (i, _scan_chunk(d, s, nc, n), 0))
    args = (u, w, qg, kd, attn, el)
    return pl.pallas_call(
        _dn_scan_kernel,
        out_shape=[jax.ShapeDtypeStruct((b, t, hw), F32)] * 2,
        grid=(b, n),
        in_specs=specs(0) + specs(1),
        out_specs=[out(0), out(1)],
        scratch_shapes=[pltpu.VMEM((2 * DN_HEADS, DN_HEAD_DIM, DN_HEAD_DIM), F32)],
        compiler_params=_cparams("parallel", "arbitrary"),
        name="deltanet_scan",
    )(*args, *args)


def _gated_norm_kernel(oa_ref, ob_ref, z_ref, gain_ref, out_ref):
    o = oa_ref[...] + ob_ref[...]
    z = z_ref[...].astype(F32)
    for j in range(0, o.shape[1], LANES):
        oh = _rms(o[:, j:j + LANES]) * gain_ref[...]
        out_ref[:, j:j + LANES] = (oh * _silu(z[:, j:j + LANES])).astype(out_ref.dtype)


def _gated_norm(o_fwd, o_bwd, z, gain_row):
    b, t, w = z.shape
    o_spec = lambda d: pl.BlockSpec((None, ROW_TILE, w), lambda i, j: (i, j, 0))
    return pl.pallas_call(
        _gated_norm_kernel,
        out_shape=jax.ShapeDtypeStruct((b, t, w), BF16),
        grid=(b, t // ROW_TILE),
        in_specs=[o_spec(0), o_spec(1),
                  pl.BlockSpec((None, ROW_TILE, w), lambda i, j: (i, j, 0)),
                  pl.BlockSpec((1, LANES), lambda i, j: (0, 0))],
        out_specs=pl.BlockSpec((None, ROW_TILE, w), lambda i, j: (i, j, 0)),
        compiler_params=_cparams("parallel", "parallel"),
        name="gated_norm",
    )(o_fwd, o_bwd, z, gain_row)


def _merge_kernel(h_ref, oa_ref, ob_ref, oc_ref, od_ref, wg_ref, wb_ref, wo_ref, x_ref, mod_ref, gains_ref,
                  wr_ref, br_ref, xn_ref, h2_ref, idx_ref, wgt_ref):
    d = x_ref.shape[1]
    h = h_ref[...]
    merged = None
    for n, o_ref in enumerate((oa_ref, ob_ref, oc_ref, od_ref)):
        gate = jax.nn.sigmoid(_dot(h, wg_ref[:, n * d:(n + 1) * d]))
        term = gate * _dot(o_ref[...], wb_ref[n])
        merged = term if merged is None else merged + term
    y = _dot(merged.astype(BF16), wo_ref[...])
    x_new = x_ref[...] + mod_ref[2:3, :] * (_rms(y) * gains_ref[1:2, :])
    xn_ref[...] = x_new
    h2 = _rms(x_new) * gains_ref[2:3, :] * (1.0 + mod_ref[4:5, :]) + mod_ref[3:4, :]
    for s in range(d // LANES):
        h2_ref[:, s, :] = h2[:, s * LANES:(s + 1) * LANES]
    logits = _dot_hi(h2, wr_ref[...]) + br_ref[...]
    ne = logits.shape[1]
    lane = lax.broadcasted_iota(jnp.int32, logits.shape, 1)
    vals = []
    for kk in range(TOP_K):
        m = jnp.max(logits, axis=-1, keepdims=True)
        sel = jnp.min(jnp.where(logits == m, lane, ne), axis=-1, keepdims=True)
        sel = jnp.minimum(sel, ne - 1)
        idx_ref[:, kk:kk + 1] = sel
        vals.append(m)
        logits = jnp.where(lane == sel, -jnp.inf, logits)
    es = [jnp.exp(vv - vals[0]) for vv in vals]
    tot = es[0] + es[1] + es[2] + es[3]
    for kk in range(TOP_K):
        wgt_ref[:, kk:kk + 1] = es[kk] / tot


def _merge(h, branch_outs, wg, wb, wo, x, modtab, gains, w_router, b_router, nct, tile0):
    b, t, d = x.shape
    nt = t // ROW_TILE - tile0
    ne = w_router.shape[1]
    rows = lambda wd: pl.BlockSpec((None, ROW_TILE, wd), lambda i, j: (i, j + tile0, 0))
    bw = branch_outs[0].shape[2]
    return pl.pallas_call(
        _merge_kernel,
        out_shape=[jax.ShapeDtypeStruct((b, t, d), F32),
                   jax.ShapeDtypeStruct((b, t, d // LANES, LANES), F32),
                   jax.ShapeDtypeStruct((b, t, TOP_K), jnp.int32),
                   jax.ShapeDtypeStruct((b, t, TOP_K), F32)],
        grid=(b, nt),
        in_specs=[rows(d), rows(bw), rows(bw), rows(bw), rows(bw),
                  pl.BlockSpec(wg.shape, lambda i, j: (0, 0), pipeline_mode=pl.Buffered(1)),
                  pl.BlockSpec(wb.shape, lambda i, j: (0, 0, 0), pipeline_mode=pl.Buffered(1)),
                  pl.BlockSpec(wo.shape, lambda i, j: (0, 0), pipeline_mode=pl.Buffered(1)),
                  rows(d),
                  pl.BlockSpec((None, None, N_MOD, d), lambda i, j: (i, (j + tile0 >= nct).astype(jnp.int32), 0, 0)),
                  pl.BlockSpec(gains.shape, lambda i, j: (0, 0)),
                  pl.BlockSpec(w_router.shape, lambda i, j: (0, 0)),
                  pl.BlockSpec((1, ne), lambda i, j: (0, 0))],
        out_specs=[rows(d),
                   pl.BlockSpec((None, ROW_TILE, d // LANES, LANES), lambda i, j: (i, j + tile0, 0, 0)),
                   rows(TOP_K), rows(TOP_K)],
        compiler_params=_cparams("parallel", "parallel"),
        name="merge",
    )(h, *branch_outs, wg, wb, wo, x, modtab, gains, w_router, b_router.reshape(1, ne))


def _rank_kernel(idx_ref, rank_ref, count_ref, carry_ref, *, ne):
    @pl.when(pl.program_id(0) == 0)
    def _():
        carry_ref[...] = jnp.zeros_like(carry_ref)

    idx = idx_ref[...]
    tm = idx.shape[0]
    lane = lax.broadcasted_iota(jnp.int32, (tm, ne), 1)
    onehots = [(lane == idx[:, kk:kk + 1]) for kk in range(TOP_K)]
    member = onehots[0] | onehots[1] | onehots[2] | onehots[3]
    i = lax.broadcasted_iota(jnp.int32, (tm, tm), 0)
    j = lax.broadcasted_iota(jnp.int32, (tm, tm), 1)
    before = _dot((j < i).astype(BF16), member.astype(BF16)) + carry_ref[...]
    for kk in range(TOP_K):
        rk = jnp.sum(jnp.where(onehots[kk], before, 0.0), axis=-1, keepdims=True)
        rank_ref[:, kk:kk + 1] = rk.astype(jnp.int32)
    carry_ref[...] = carry_ref[...] + jnp.sum(member.astype(F32), axis=0, keepdims=True)
    count_ref[...] = carry_ref[...].astype(jnp.int32)


def _moe_rank(top_idx, ne):
    n = top_idx.shape[0]
    return pl.pallas_call(
        functools.partial(_rank_kernel, ne=ne),
        out_shape=[jax.ShapeDtypeStruct((n, TOP_K), jnp.int32), jax.ShapeDtypeStruct((1, ne), jnp.int32)],
        grid=(n // ROW_TILE,),
        in_specs=[pl.BlockSpec((ROW_TILE, TOP_K), lambda i: (i, 0))],
        out_specs=[pl.BlockSpec((ROW_TILE, TOP_K), lambda i: (i, 0)), pl.BlockSpec((1, ne), lambda i: (0, 0))],
        scratch_shapes=[pltpu.VMEM((1, ne), F32)],
        compiler_params=_cparams("arbitrary"),
        name="moe_rank",
    )(top_idx)


def _row_copy(src, dst, sem):
    return pltpu.make_async_copy(src, dst, sem)


def _dispatch_kernel(dest_ref, h_ref, buf_in, buf_hbm, sem):
    del buf_in

    def issue(r, carry):
        for kk in range(TOP_K):
            _row_copy(h_ref.at[r], buf_hbm.at[dest_ref[0, r * TOP_K + kk]], sem).start()
        return carry

    lax.fori_loop(0, ROW_TILE, issue, 0)
    _row_copy(buf_hbm.at[pl.ds(0, ROW_TILE * TOP_K)], buf_hbm.at[pl.ds(0, ROW_TILE * TOP_K)], sem).wait()


def _moe_dispatch(h2, dest, n_rows):
    n, s, l = h2.shape
    buf0 = jnp.zeros((n_rows, s, l), h2.dtype)
    dest2 = dest.reshape(n // ROW_TILE, 1, ROW_TILE * TOP_K)
    return pl.pallas_call(
        _dispatch_kernel,
        out_shape=jax.ShapeDtypeStruct((n_rows, s, l), h2.dtype),
        grid=(n // ROW_TILE,),
        in_specs=[pl.BlockSpec((None, 1, ROW_TILE * TOP_K), lambda i: (i, 0, 0), memory_space=pltpu.SMEM),
                  pl.BlockSpec((ROW_TILE, s, l), lambda i: (i, 0, 0)),
                  pl.BlockSpec(memory_space=pl.ANY)],
        out_specs=pl.BlockSpec(memory_space=pl.ANY),
        scratch_shapes=[pltpu.SemaphoreType.DMA],
        input_output_aliases={2: 0},
        compiler_params=pltpu.CompilerParams(dimension_semantics=("arbitrary",), vmem_limit_bytes=VMEM_LIMIT,
                                             has_side_effects=True),
        name="moe_dispatch",
    )(dest2, h2, buf0)


def _ffn_kernel(te_ref, nu_ref, x_ref, wgu_ref, bgu_ref, wd_ref, bd_ref, y_ref):
    i = pl.program_id(0)
    ns = x_ref.shape[1]

    @pl.when(i < nu_ref[0])
    def _():
        subs = [pl.ds(h * MOE_SUB, MOE_SUB) for h in range(MOE_TILE // MOE_SUB)]
        gus = []
        for rows in subs:
            x = jnp.concatenate([x_ref[rows, s, :] for s in range(ns)], axis=-1).astype(BF16)
            gus.append(_dot(x, wgu_ref[...]) + bgu_ref[...])
        for rows, gu in zip(subs, gus):
            f = gu.shape[1] // 2
            gate = jnp.minimum(gu[:, :f], SWIGLU_LIMIT)
            up = jnp.clip(gu[:, f:], -SWIGLU_LIMIT, SWIGLU_LIMIT)
            act = (up + 1.0) * gate * jax.nn.sigmoid(SWIGLU_ALPHA * gate)
            y = _dot(act.astype(BF16), wd_ref[...]) + bd_ref[...]
            for s in range(ns):
                y_ref[rows, s, :] = y[:, s * LANES:(s + 1) * LANES]

    @pl.when(i >= nu_ref[0])
    def _():
        y_ref[...] = jnp.zeros_like(y_ref)


def _moe_ffn(buf, tile_expert, n_used, wgu, bgu, wd, bd):
    n_rows, s, l = buf.shape
    ne, d, f2 = wgu.shape
    grid_spec = pltpu.PrefetchScalarGridSpec(
        num_scalar_prefetch=2,
        grid=(n_rows // MOE_TILE,),
        in_specs=[pl.BlockSpec((MOE_TILE, s, l), lambda i, te, nu: (i, 0, 0)),
                  pl.BlockSpec((None, d, f2), lambda i, te, nu: (te[i], 0, 0)),
                  pl.BlockSpec((None, 1, f2), lambda i, te, nu: (te[i], 0, 0)),
                  pl.BlockSpec((None, f2 // 2, d), lambda i, te, nu: (te[i], 0, 0)),
                  pl.BlockSpec((None, 1, d), lambda i, te, nu: (te[i], 0, 0))],
        out_specs=pl.BlockSpec((MOE_TILE, s, l), lambda i, te, nu: (i, 0, 0)),
    )
    return pl.pallas_call(
        _ffn_kernel,
        out_shape=jax.ShapeDtypeStruct((n_rows, s, l), F32),
        grid_spec=grid_spec,
        compiler_params=_cparams("arbitrary"),
        name="moe_ffn",
    )(tile_expert, n_used, buf, wgu, bgu.reshape(ne, 1, f2), wd, bd.reshape(ne, 1, d))


def _combine_kernel(dest_ref, wgt_ref, y_hbm, x_ref, mod_ref, gain_ref, o_ref, gat_ref, acc_ref, sem):
    def issue(r, carry):
        for kk in range(TOP_K):
            _row_copy(y_hbm.at[dest_ref[0, r * TOP_K + kk]], gat_ref.at[r * TOP_K + kk], sem).start()
        return carry

    lax.fori_loop(0, ROW_TILE, issue, 0)
    _row_copy(y_hbm.at[pl.ds(0, ROW_TILE * TOP_K)], gat_ref, sem).wait()

    def mix(r, carry):
        acc = None
        for kk in range(TOP_K):
            term = wgt_ref[0, r * TOP_K + kk] * gat_ref[r * TOP_K + kk]
            acc = term if acc is None else acc + term
        acc_ref[r] = acc
        return carry

    lax.fori_loop(0, ROW_TILE, mix, 0)
    f = jnp.concatenate([acc_ref[:, s, :] for s in range(acc_ref.shape[1])], axis=-1)
    o_ref[...] = x_ref[...] + mod_ref[5:6, :] * (_rms(f) * gain_ref[...])


def _moe_combine(ybuf, dest, wgt, x, modtab, gain_row, nct, tile0):
    b, t, d = x.shape
    tpb = t // ROW_TILE
    nt = tpb - tile0
    s, l = ybuf.shape[1:]
    dest2 = dest.reshape(b * tpb, 1, ROW_TILE * TOP_K)
    wgt2 = wgt.reshape(b * tpb, 1, ROW_TILE * TOP_K)
    smem_rows = lambda: pl.BlockSpec((None, 1, ROW_TILE * TOP_K), lambda i, j: (i * tpb + j + tile0, 0, 0),
                                     memory_space=pltpu.SMEM)
    return pl.pallas_call(
        _combine_kernel,
        out_shape=jax.ShapeDtypeStruct((b, t, d), F32),
        grid=(b, nt),
        in_specs=[smem_rows(), smem_rows(),
                  pl.BlockSpec(memory_space=pl.ANY),
                  pl.BlockSpec((None, ROW_TILE, d), lambda i, j: (i, j + tile0, 0)),
                  pl.BlockSpec((None, None, N_MOD, d), lambda i, j: (i, (j + tile0 >= nct).astype(jnp.int32), 0, 0)),
                  pl.BlockSpec((1, d), lambda i, j: (0, 0))],
        out_specs=pl.BlockSpec((None, ROW_TILE, d), lambda i, j: (i, j + tile0, 0)),
        scratch_shapes=[pltpu.VMEM((ROW_TILE * TOP_K, s, l), F32),
                        pltpu.VMEM((ROW_TILE, s, l), F32),
                        pltpu.SemaphoreType.DMA],
        compiler_params=_cparams("arbitrary", "arbitrary"),
        name="moe_combine",
    )(dest2, wgt2, ybuf, x, modtab, gain_row)


def _deinterleave(width):
    idx = np.arange(width).reshape(-1, HEAD_DIM // 2, 2)
    return np.concatenate([idx[:, :, 0], idx[:, :, 1]], axis=1).reshape(-1)


def _rope_tables(ang, ctx_len):
    cos = jnp.cos(ang)
    sin = jnp.sin(ang)
    cos64 = jnp.concatenate([cos, cos], axis=-1)
    sin64 = jnp.concatenate([-sin, sin], axis=-1)
    reps = LANES // HEAD_DIM
    cos_t = jnp.concatenate([jnp.ones((ctx_len, HEAD_DIM), F32), cos64], axis=0)
    sin_t = jnp.concatenate([jnp.zeros((ctx_len, HEAD_DIM), F32), sin64], axis=0)
    return jnp.tile(cos_t, (1, reps)), jnp.tile(sin_t, (1, reps))


def _axial_angles(n_tok):
    rows = n_tok // GRID_W
    row = jnp.repeat(jnp.arange(rows), GRID_W).astype(F32)
    col = jnp.tile(jnp.arange(GRID_W), rows).astype(F32)
    half = HEAD_DIM // 2
    inv = ROPE_THETA ** (-jnp.arange(0, half, 2, dtype=F32) / half)
    return jnp.concatenate([row[:, None] * inv, col[:, None] * inv], axis=-1)


def _line_angles(n_tok):
    pos = jnp.arange(n_tok, dtype=F32)
    inv = ROPE_THETA ** (-jnp.arange(0, RET_QK_DIM, 2, dtype=F32) / RET_QK_DIM)
    return pos[:, None] * inv


def _layer(x, modtab, p, layer_idx, ctx_len, last):
    b, t, d = x.shape
    nct = ctx_len // ROW_TILE
    del last
    tile0 = 0
    bwid = d // 2
    sizes = (bwid, GQA_KV_HEADS * HEAD_DIM, GQA_KV_HEADS * HEAD_DIM,
             bwid, bwid, bwid,
             RET_HEADS * RET_QK_DIM, RET_HEADS * RET_QK_DIM, bwid, bwid,
             3 * bwid, bwid, 2 * DN_HEADS, 2 * DN_HEADS,
             N_BRANCHES * d)
    cuts = np.concatenate([[0], np.cumsum(sizes)])
    w_in = p['w_in']
    col = lambda a, e: w_in[:, cuts[a]:cuts[e]]
    gains = p['norm_gain']
    ones_row = jnp.ones((1, LANES), F32)

    h = _modulate(x, modtab, gains[0:1], nct)

    seq = t - ctx_len
    cos_ax, sin_ax = _rope_tables(_axial_angles(seq), ctx_len)
    cos_ln, sin_ln = _rope_tables(_line_angles(seq), ctx_len)

    perm_q = _deinterleave(sizes[0])
    perm_k = _deinterleave(sizes[1])
    wa = jnp.concatenate([col(0, 1)[:, perm_q], col(1, 2)[:, perm_k], col(2, 3)], axis=1).astype(BF16)
    perm64 = _deinterleave(HEAD_DIM)
    qk_gain = p['gqa_qk_gain'].astype(F32)
    gain_a = jnp.concatenate([jnp.tile(qk_gain[0][perm64], sizes[0] // HEAD_DIM),
                              jnp.tile(qk_gain[1][perm64], sizes[1] // HEAD_DIM)]).reshape(1, -1)
    qa, ka, va = _inproj(h, wa, cos_ax, sin_ax, gain_a,
                         [(sizes[0], BF16), (sizes[1], BF16), (sizes[2], BF16)], sizes[0] + sizes[1], True)
    oa = _gqa_attention(qa, ka, va, nct, ctx_len, tile0)

    perm_b = _deinterleave(bwid)
    wb_in = jnp.concatenate([col(3, 4)[:, perm_b], col(4, 5)[:, perm_b], col(5, 6)], axis=1).astype(BF16)
    qb, kb, vb = _inproj(h, wb_in, cos_ax, sin_ax, ones_row,
                         [(bwid, BF16)] * 3, 2 * bwid, False)
    lam_init = 0.8 - 0.6 * math.exp(-0.3 * layer_idx)
    ob = _diff_attention(qb, kb, vb, p['diff_lambda'].astype(F32), p['diff_norm'].astype(F32).reshape(1, -1),
                         nct, ctx_len, tile0, lam_init)

    perm_c = _deinterleave(sizes[6])
    wc = jnp.concatenate([col(6, 7)[:, perm_c], col(7, 8)[:, perm_c] * (RET_QK_DIM ** -0.5), col(8, 10)],
                         axis=1).astype(BF16)
    qc, kc, vc, gc = _inproj(h, wc, cos_ln, sin_ln, ones_row,
                             [(sizes[6], BF16), (sizes[7], BF16), (bwid, BF16), (bwid, BF16)],
                             sizes[6] + sizes[7], False)
    log_gamma = jax.nn.log_sigmoid(p['ret_decay_logit'].astype(F32))
    oc_f, oc_b = _retention(qc, kc, vc, log_gamma, ctx_len)
    oc = _gated_norm(oc_f, oc_b, gc, ones_row)

    wd_in = col(10, 12).astype(BF16)
    w_ba = col(12, 14).astype(BF16)
    w_ba = jnp.pad(w_ba, ((0, 0), (0, LANES - w_ba.shape[1])))
    wd_all = jnp.concatenate([wd_in, w_ba], axis=1)
    qkv_d, z_d, ba_d = _inproj(h, wd_all, cos_ln, sin_ln, ones_row,
                               [(3 * bwid, F32), (bwid, BF16), (LANES, F32)], 0, False)
    nb = 2 * DN_HEADS
    rate_row = jnp.exp(p['dn_a_log'].astype(F32)).reshape(1, nb)
    bias_row = p['dn_dt_bias'].astype(F32).reshape(1, nb)
    qd, kd, vd, g_d, beta_d = _dn_prep(qkv_d, ba_d[:, :, :2 * nb], p['dn_conv_w'].astype(F32), rate_row, bias_row, nct)
    od_f, od_b = _dn_scan(*_dn_local(qd, kd, vd, g_d, beta_d), ctx_len)
    od = _gated_norm(od_f, od_b, z_d, p['dn_norm'].astype(F32).reshape(1, -1))

    wg = col(14, 15).astype(BF16)
    x_new, h2, top_idx, top_w = _merge(h, (oa, ob, oc, od), wg, p['w_branch'].astype(BF16),
                                       p['w_out'].astype(BF16), x, modtab, gains,
                                       p['w_router'].astype(F32), p['b_router'].astype(F32), nct, tile0)

    ne = p['w_router'].shape[1]
    n = b * t
    top_idx = top_idx.reshape(n, TOP_K)
    rank, counts = _moe_rank(top_idx, ne)
    counts = counts.reshape(ne)
    padded = (counts + MOE_TILE - 1) // MOE_TILE * MOE_TILE
    pad_ends = jnp.cumsum(padded)
    pad_starts = pad_ends - padded
    dest = pad_starts[top_idx] + rank
    n_tiles = (n * TOP_K) // MOE_TILE + ne
    tile_expert = jnp.minimum(jnp.searchsorted(pad_ends, jnp.arange(n_tiles) * MOE_TILE, side='right'),
                              ne - 1).astype(jnp.int32)
    n_used = (pad_ends[-1] // MOE_TILE).astype(jnp.int32).reshape(1)
    buf = _moe_dispatch(h2.reshape(n, d // LANES, LANES), dest, n_tiles * MOE_TILE)
    ybuf = _moe_ffn(buf, tile_expert, n_used, p['w_gate_up'].astype(BF16), p['b_gate_up'].astype(F32),
                    p['w_down'].astype(BF16), p['b_down'].astype(F32))
    return _moe_combine(ybuf, dest, top_w.reshape(n, TOP_K), x_new, modtab, gains[3:4], nct, tile0)


def kernel(x, c, ctx, c_ctx, w_mod, b_mod, norm_gain, w_in, gqa_qk_gain, diff_lambda, diff_norm, ret_decay_logit,
           dn_conv_w, dn_a_log, dn_dt_bias, dn_norm, w_branch, w_out, w_router, b_router, w_gate_up, b_gate_up,
           w_down, b_down):
    b, seq, d = x.shape
    ctx_len = ctx.shape[1]
    depth = w_mod.shape[0]
    assert ctx_len % ROW_TILE == 0 and seq % ROW_TILE == 0 and seq % GRID_W == 0 and d % LANES == 0
    xa = jnp.concatenate([ctx, x], axis=1).astype(F32)
    c_all = jnp.concatenate([c, c_ctx[None, :]], axis=0).astype(F32)
    for l in range(depth):
        p = {
            'norm_gain': norm_gain[l].astype(F32), 'w_in': w_in[l], 'gqa_qk_gain': gqa_qk_gain[l],
            'diff_lambda': diff_lambda[l], 'diff_norm': diff_norm[l], 'ret_decay_logit': ret_decay_logit[l],
            'dn_conv_w': dn_conv_w[l], 'dn_a_log': dn_a_log[l], 'dn_dt_bias': dn_dt_bias[l], 'dn_norm': dn_norm[l],
            'w_branch': w_branch[l], 'w_out': w_out[l], 'w_router': w_router[l], 'b_router': b_router[l],
            'w_gate_up': w_gate_up[l], 'b_gate_up': b_gate_up[l], 'w_down': w_down[l], 'b_down': b_down[l],
        }
        mod = _mod_table(c_all, w_mod[l].astype(F32), b_mod[l].astype(F32))
        mod = mod.reshape(b + 1, N_MOD, d)
        modtab = jnp.stack([jnp.broadcast_to(mod[b], (b, N_MOD, d)), mod[:b]], axis=1)
        xa = _layer(xa, modtab, p, l, ctx_len, l == depth - 1)
    return xa[:, ctx_len:, :]
```

```python
import functools
import math

import numpy as np
import jax
import jax.numpy as jnp
from jax import lax
from jax.experimental import pallas as pl
from jax.experimental.pallas import tpu as pltpu

F32 = jnp.float32
BF16 = jnp.bfloat16

GRID_W = 64
RMS_EPS = 1e-6
ROPE_THETA = 10000.0
HEAD_DIM = 64
GQA_KV_HEADS = 2
GQA_GROUP = 4
DIFF_HEADS = 4
RET_HEADS = 4
RET_QK_DIM = 64
RET_V_DIM = 128
DN_HEADS = 4
DN_HEAD_DIM = 128
DN_CONV = 5
DN_CHUNK = 64
TOP_K = 4
SWIGLU_LIMIT = 7.0
SWIGLU_ALPHA = 1.702
N_MOD = 6
N_BRANCHES = 4

LANES = 128
SUBLANES = 8
ROW_TILE = 256
RET_CHUNK = 128
MERGE_SUB = 128
ATT_KEY_BLOCK = 256
ATT_SKEW = 3
DN_BATCH_CHUNKS = 4
MOE_TILE = 512
MOE_SUB = 256
VMEM_LIMIT = 56 * 1024 * 1024


def _cparams(*sem):
    return pltpu.CompilerParams(dimension_semantics=sem, vmem_limit_bytes=VMEM_LIMIT)


def _const_spec(shape):
    nd = len(shape)
    return pl.BlockSpec(shape, lambda *_: (0,) * nd)


def _rms(x):
    return x * lax.rsqrt(jnp.mean(x * x, axis=-1, keepdims=True) + RMS_EPS)


def _silu(x):
    return x * jax.nn.sigmoid(x)


def _dot(a, b):
    return jnp.dot(a, b, preferred_element_type=F32)


def _dot_nt(a, b):
    return lax.dot_general(a, b, (((1,), (1,)), ((), ())), preferred_element_type=F32)


def _dot_tn(a, b):
    return lax.dot_general(a, b, (((0,), (0,)), ((), ())), preferred_element_type=F32)


def _mod_kernel(c_ref, w_ref, b_ref, o_ref):
    o_ref[...] = _dot(_silu(c_ref[...]), w_ref[...]) + b_ref[...]


def _mod_table(c_all, w_mod, b_mod):
    m, d = c_all.shape
    n = w_mod.shape[1]
    tn = d
    return pl.pallas_call(
        _mod_kernel,
        out_shape=jax.ShapeDtypeStruct((m, n), F32),
        grid=(n // tn,),
        in_specs=[pl.BlockSpec((m, d), lambda j: (0, 0)),
                  pl.BlockSpec((d, tn), lambda j: (0, j)),
                  pl.BlockSpec((1, tn), lambda j: (0, j))],
        out_specs=pl.BlockSpec((m, tn), lambda j: (0, j)),
        compiler_params=_cparams("parallel"),
        name="mod_table",
    )(c_all, w_mod, b_mod.reshape(1, n))


def _modulate_kernel(x_ref, mod_ref, gain_ref, h_ref):
    xn = _rms(x_ref[...]) * gain_ref[...]
    h = xn * (1.0 + mod_ref[1:2, :]) + mod_ref[0:1, :]
    h_ref[...] = h.astype(h_ref.dtype)


def _modulate(x, modtab, gain_row, nct):
    b, t, d = x.shape
    return pl.pallas_call(
        _modulate_kernel,
        out_shape=jax.ShapeDtypeStruct((b, t, d), BF16),
        grid=(b, t // ROW_TILE),
        in_specs=[pl.BlockSpec((None, ROW_TILE, d), lambda i, j: (i, j, 0)),
                  pl.BlockSpec((None, None, N_MOD, d), lambda i, j: (i, (j >= nct).astype(jnp.int32), 0, 0)),
                  pl.BlockSpec((1, d), lambda i, j: (0, 0))],
        out_specs=pl.BlockSpec((None, ROW_TILE, d), lambda i, j: (i, j, 0)),
        compiler_params=_cparams("parallel", "parallel"),
        name="modulate",
    )(x, modtab, gain_row)


def _swap_halves(y):
    lane = lax.broadcasted_iota(jnp.int32, y.shape, 1)
    first = (lane % HEAD_DIM) < (HEAD_DIM // 2)
    return jnp.where(first, pltpu.roll(y, LANES - HEAD_DIM // 2, 1), pltpu.roll(y, HEAD_DIM // 2, 1))


def _inproj_kernel(h_ref, w_ref, cos_ref, sin_ref, gain_ref, seg_ref, *out_refs, widths, n_rope, do_norm):
    u = _dot(h_ref[...], w_ref[...])
    cos = cos_ref[...]
    sin = sin_ref[...]
    col = 0
    for o_ref, wd in zip(out_refs, widths):
        for j in range(0, wd, LANES):
            wj = min(LANES, wd - j)
            y = u[:, col + j: col + j + wj]
            if col + j < n_rope:
                if do_norm:
                    ms = _dot((y * y).astype(BF16), seg_ref[...])
                    y = y * lax.rsqrt(ms + RMS_EPS) * gain_ref[:, col + j: col + j + wj]
                y = y * cos + _swap_halves(y) * sin
            o_ref[:, j: j + wj] = y.astype(o_ref.dtype)
        col += wd


def _inproj(h, w, cos, sin, gain_row, outs, n_rope, do_norm):
    b, t, d = h.shape
    c = w.shape[1]
    widths = tuple(o[0] for o in outs)
    assert sum(widths) == c
    seg = np.kron(np.eye(LANES // HEAD_DIM), np.full((HEAD_DIM, HEAD_DIM), 1.0 / HEAD_DIM))
    seg = jnp.asarray(seg, BF16)
    kern = functools.partial(_inproj_kernel, widths=widths, n_rope=n_rope, do_norm=do_norm)
    return pl.pallas_call(
        kern,
        out_shape=[jax.ShapeDtypeStruct((b, t, wd), dt) for wd, dt in outs],
        grid=(b, t // ROW_TILE),
        in_specs=[pl.BlockSpec((None, ROW_TILE, d), lambda i, j: (i, j, 0)),
                  pl.BlockSpec((d, c), lambda i, j: (0, 0)),
                  pl.BlockSpec((ROW_TILE, LANES), lambda i, j: (j, 0)),
                  pl.BlockSpec((ROW_TILE, LANES), lambda i, j: (j, 0)),
                  pl.BlockSpec(gain_row.shape, lambda i, j: (0, 0)),
                  pl.BlockSpec((LANES, LANES), lambda i, j: (0, 0))],
        out_specs=[pl.BlockSpec((None, ROW_TILE, wd), lambda i, j: (i, j, 0)) for wd, _ in outs],
        compiler_params=_cparams("parallel", "parallel"),
        name="inproj",
    )(h, w, cos, sin, gain_row, seg)


def _softmax_attend(n_maps, nk, query, keys, values, emit, s_ref):
    kb = min(ATT_KEY_BLOCK, nk)
    blocks = [slice(j * kb, (j + 1) * kb) for j in range(nk // kb)]
    rows = s_ref.shape[1]
    lane_pieces = lambda a: [a[:, c:c + LANES] for c in range(0, kb, LANES)]

    def pass1_block(n, q, blk, m_run):
        s = _dot_nt(q, keys(n, blk))
        s_ref[n % 2, :, blk] = s
        for piece in lane_pieces(s):
            m_run = jnp.maximum(m_run, piece)
        return m_run

    neg = jnp.full((rows, LANES), -jnp.inf, F32)
    q_next = query(0)
    m_run = neg
    for blk in blocks:
        m_run = pass1_block(0, q_next, blk, m_run)
    for n in range(n_maps):
        m = jnp.max(m_run, axis=-1, keepdims=True)
        m_run = neg
        if n + 1 < n_maps:
            q_next = query(n + 1)
        l_run = jnp.zeros((rows, LANES), F32)
        acc = None
        ahead = blocks[:ATT_SKEW] if n + 1 < n_maps else []
        for blk in ahead:
            m_run = pass1_block(n + 1, q_next, blk, m_run)
        for idx, blk in enumerate(blocks):
            if n + 1 < n_maps and idx + ATT_SKEW < len(blocks):
                m_run = pass1_block(n + 1, q_next, blocks[idx + ATT_SKEW], m_run)
            p = jnp.exp(s_ref[n % 2, :, blk] - m)
            for piece in lane_pieces(p):
                l_run = l_run + piece
            pv = _dot(p.astype(BF16), values(n, blk))
            acc = pv if acc is None else acc + pv
        emit(n, acc / jnp.sum(l_run, axis=-1, keepdims=True))


def _gqa_kernel(q_ref, k_ref, v_ref, o_ref, s_ref, *, nct, ctx_len, tile0):
    t = pl.program_id(1) + tile0

    def run(nk):
        head = lambda n: slice(n * HEAD_DIM, (n + 1) * HEAD_DIM)
        kv_head = lambda n: head(n // GQA_GROUP)

        def emit(n, o):
            o_ref[:, head(n)] = o.astype(o_ref.dtype)

        _softmax_attend(GQA_KV_HEADS * GQA_GROUP, nk,
                        lambda n: q_ref[:, head(n)],
                        lambda n, blk: k_ref[blk, kv_head(n)],
                        lambda n, blk: v_ref[blk, kv_head(n)],
                        emit, s_ref)

    if tile0 < nct:
        @pl.when(t < nct)
        def _():
            run(ctx_len)

    @pl.when(t >= nct)
    def _():
        run(k_ref.shape[0])


def _gqa_attention(q, k, v, nct, ctx_len, tile0):
    b, t, cq = q.shape
    nt = t // ROW_TILE - tile0
    kern = functools.partial(_gqa_kernel, nct=nct, ctx_len=ctx_len, tile0=tile0)
    return pl.pallas_call(
        kern,
        out_shape=jax.ShapeDtypeStruct((b, t, cq), BF16),
        grid=(b, nt),
        in_specs=[pl.BlockSpec((None, ROW_TILE, cq), lambda i, j: (i, j + tile0, 0)),
                  pl.BlockSpec((None, t, k.shape[2]), lambda i, j: (i, 0, 0)),
                  pl.BlockSpec((None, t, v.shape[2]), lambda i, j: (i, 0, 0))],
        out_specs=pl.BlockSpec((None, ROW_TILE, cq), lambda i, j: (i, j + tile0, 0)),
        scratch_shapes=[pltpu.VMEM((2, ROW_TILE, t), F32)],
        compiler_params=_cparams("parallel", "parallel"),
        name="gqa_attention",
    )(q, k, v)


def _diff_kernel(q_ref, k_ref, v_ref, lam_ref, gain_ref, o_ref, s_ref, *, nct, ctx_len, tile0, lam_init):
    t = pl.program_id(1) + tile0
    lp = lam_ref[...]
    lam = (jnp.exp(jnp.sum(lp[0:1, :] * lp[1:2, :], axis=-1, keepdims=True))
           - jnp.exp(jnp.sum(lp[2:3, :] * lp[3:4, :], axis=-1, keepdims=True)) + lam_init)
    dv = 2 * HEAD_DIM

    def run(nk):
        qk_cols = lambda n: slice(n * HEAD_DIM, (n + 1) * HEAD_DIM)
        v_cols = lambda n: slice((n // 2) * dv, (n // 2 + 1) * dv)
        first = {}

        def emit(n, o):
            if n % 2 == 0:
                first[n // 2] = o
            else:
                out = _rms(first.pop(n // 2) - lam * o) * gain_ref[...] * (1.0 - lam_init)
                o_ref[:, v_cols(n)] = out.astype(o_ref.dtype)

        _softmax_attend(2 * DIFF_HEADS, nk,
                        lambda n: q_ref[:, qk_cols(n)],
                        lambda n, blk: k_ref[blk, qk_cols(n)],
                        lambda n, blk: v_ref[blk, v_cols(n)],
                        emit, s_ref)

    if tile0 < nct:
        @pl.when(t < nct)
        def _():
            run(ctx_len)

    @pl.when(t >= nct)
    def _():
        run(k_ref.shape[0])


def _diff_attention(q, k, v, lam_params, gain_row, nct, ctx_len, tile0, lam_init):
    b, t, cq = q.shape
    nt = t // ROW_TILE - tile0
    kern = functools.partial(_diff_kernel, nct=nct, ctx_len=ctx_len, tile0=tile0, lam_init=lam_init)
    return pl.pallas_call(
        kern,
        out_shape=jax.ShapeDtypeStruct((b, t, cq), BF16),
        grid=(b, nt),
        in_specs=[pl.BlockSpec((None, ROW_TILE, cq), lambda i, j: (i, j + tile0, 0)),
                  pl.BlockSpec((None, t, cq), lambda i, j: (i, 0, 0)),
                  pl.BlockSpec((None, t, cq), lambda i, j: (i, 0, 0)),
                  pl.BlockSpec(lam_params.shape, lambda i, j: (0, 0)),
                  pl.BlockSpec(gain_row.shape, lambda i, j: (0, 0))],
        out_specs=pl.BlockSpec((None, ROW_TILE, cq), lambda i, j: (i, j + tile0, 0)),
        scratch_shapes=[pltpu.VMEM((2, ROW_TILE, t), F32)],
        compiler_params=_cparams("parallel", "parallel"),
        name="diff_attention",
    )(q, k, v, lam_params, gain_row)


def _scan_chunk(d, s, n_ctx_chunks, n_chunks):
    rev = jnp.where(s < n_ctx_chunks, n_ctx_chunks - 1 - s, n_chunks + n_ctx_chunks - 1 - s)
    return jnp.where(d == 0, s, rev)


def _ret_kernel(lg_ref, qf_ref, kf_ref, vf_ref, qb_ref, kb_ref, vb_ref, of_ref, ob_ref, state_ref):
    c = RET_CHUNK
    nh = RET_HEADS

    @pl.when(pl.program_id(1) == 0)
    def _():
        state_ref[...] = jnp.zeros_like(state_ref)

    i = lax.broadcasted_iota(jnp.int32, (c, c), 0)
    j = lax.broadcasted_iota(jnp.int32, (c, c), 1)
    r = lax.broadcasted_iota(jnp.int32, (c, 1), 0)
    lg = jnp.stack([jnp.full((1, 1), lg_ref[d, hd], F32) for d in range(2) for hd in range(nh)])
    rel = jnp.stack([i - j] * nh + [j - i] * nh)
    intra = jnp.where(rel >= 0, jnp.exp(lg * jnp.maximum(rel, 0).astype(F32)), 0.0)
    pos_q = jnp.stack([r + 1] * nh + [c - r] * nh).astype(F32)
    pos_k = jnp.stack([c - 1 - r] * nh + [r] * nh).astype(F32)

    def stacked(refs, width):
        return jnp.stack([ref[:, hd * width:(hd + 1) * width] for ref in refs for hd in range(nh)])

    q = stacked((qf_ref, qb_ref), RET_QK_DIM)
    k = stacked((kf_ref, kb_ref), RET_QK_DIM)
    v = stacked((vf_ref, vb_ref), RET_V_DIM)
    st = state_ref[...]
    scores = _bdot_nt(q, k) * intra
    qd = (q.astype(F32) * jnp.exp(lg * pos_q)).astype(BF16)
    o = _bdot(scores.astype(BF16), v) + _bdot(qd, st.astype(BF16))
    kd = (k.astype(F32) * jnp.exp(lg * pos_k)).astype(BF16)
    state_ref[...] = st * jnp.exp(lg * c) + _bdot_tn(kd, v)
    for n in range(2 * nh):
        d, hd = divmod(n, nh)
        (of_ref, ob_ref)[d][:, hd * RET_V_DIM:(hd + 1) * RET_V_DIM] = o[n]


def _retention(q, k, v, log_gamma, ctx_len):
    b, t, _ = q.shape
    n = t // RET_CHUNK
    nc = ctx_len // RET_CHUNK

    def specs(d):
        rows = lambda a: pl.BlockSpec((None, RET_CHUNK, a.shape[2]), lambda i, s: (i, _scan_chunk(d, s, nc, n), 0))
        return [rows(q), rows(k), rows(v)]

    return pl.pallas_call(
        _ret_kernel,
        out_shape=[jax.ShapeDtypeStruct((b, t, v.shape[2]), F32)] * 2,
        grid=(b, n),
        in_specs=[pl.BlockSpec(memory_space=pltpu.SMEM)] + specs(0) + specs(1),
        out_specs=[specs(0)[2], specs(1)[2]],
        scratch_shapes=[pltpu.VMEM((2 * RET_HEADS, RET_QK_DIM, RET_V_DIM), F32)],
        compiler_params=_cparams("parallel", "arbitrary"),
        name="retention_scan",
    )(log_gamma, q, k, v, q, k, v)


def _dn_prep_kernel(prev_ref, cur_ref, next_ref, cw_ref, ba_ref, rate_ref, bias_ref,
                    q_ref, k_ref, v_ref, g_ref, beta_ref, *, nct, n_tiles):
    t = pl.program_id(1)
    pad = DN_CONV // 2
    first = (t == 0) | (t == nct)
    last = (t == nct - 1) | (t == n_tiles - 1)
    prev = jnp.where(first, 0.0, prev_ref[...])
    nxt = jnp.where(last, 0.0, next_ref[...])
    ext = jnp.concatenate([prev, cur_ref[...], nxt], axis=0)
    rows = ext.shape[0]
    acc = None
    for tap in range(DN_CONV):
        sh = (pad - tap) % rows
        x = ext if sh == 0 else pltpu.roll(ext, sh, 0)
        term = x[SUBLANES:SUBLANES + ROW_TILE, :] * cw_ref[tap:tap + 1, :]
        acc = term if acc is None else acc + term
    y = _silu(acc)
    hw = DN_HEADS * DN_HEAD_DIM
    for hd in range(DN_HEADS):
        sl = slice(hd * DN_HEAD_DIM, (hd + 1) * DN_HEAD_DIM)
        qh = y[:, sl]
        kh = y[:, hw + hd * DN_HEAD_DIM: hw + (hd + 1) * DN_HEAD_DIM]
        q_ref[:, sl] = qh * lax.rsqrt(jnp.sum(qh * qh, axis=-1, keepdims=True) + RMS_EPS) * (DN_HEAD_DIM ** -0.5)
        k_ref[:, sl] = kh * lax.rsqrt(jnp.sum(kh * kh, axis=-1, keepdims=True) + RMS_EPS)
    v_ref[...] = y[:, 2 * hw:3 * hw]
    nb = 2 * DN_HEADS
    ba = ba_ref[...]
    beta_ref[...] = jax.nn.sigmoid(ba[:, 0:nb])
    a = ba[:, nb:2 * nb] + bias_ref[...]
    softplus = jnp.maximum(a, 0.0) + jnp.log(1.0 + jnp.exp(-jnp.abs(a)))
    g_ref[...] = -rate_ref[...] * softplus


def _dn_prep(qkv, ba, conv_w, rate_row, bias_row, nct):
    b, t, c = qkv.shape
    n_tiles = t // ROW_TILE
    hb = ROW_TILE // SUBLANES
    last_blk = t // SUBLANES - 1
    hw = DN_HEADS * DN_HEAD_DIM
    nb = 2 * DN_HEADS
    kern = functools.partial(_dn_prep_kernel, nct=nct, n_tiles=n_tiles)
    row_spec = lambda wd: pl.BlockSpec((None, ROW_TILE, wd), lambda i, j: (i, j, 0))
    return pl.pallas_call(
        kern,
        out_shape=[jax.ShapeDtypeStruct((b, t, hw), F32)] * 3 + [jax.ShapeDtypeStruct((b, t, nb), F32)] * 2,
        grid=(b, n_tiles),
        in_specs=[pl.BlockSpec((None, SUBLANES, c), lambda i, j: (i, jnp.maximum(j * hb - 1, 0), 0)),
                  row_spec(c),
                  pl.BlockSpec((None, SUBLANES, c), lambda i, j: (i, jnp.minimum((j + 1) * hb, last_blk), 0)),
                  pl.BlockSpec(conv_w.shape, lambda i, j: (0, 0)),
                  row_spec(2 * nb),
                  pl.BlockSpec((1, nb), lambda i, j: (0, 0)),
                  pl.BlockSpec((1, nb), lambda i, j: (0, 0))],
        out_specs=[row_spec(hw)] * 3 + [row_spec(nb)] * 2,
        compiler_params=_cparams("parallel", "parallel"),
        name="deltanet_prep",
    )(qkv, qkv, qkv, conv_w, ba, rate_row, bias_row)


def _split2(x):
    hi = x.astype(BF16)
    lo = (x - hi.astype(F32)).astype(BF16)
    return hi, lo


def _bdot(a, b):
    return lax.dot_general(a, b, (((2,), (1,)), ((0,), (0,))), preferred_element_type=F32)


def _bdot_nt(a, b):
    return lax.dot_general(a, b, (((2,), (2,)), ((0,), (0,))), preferred_element_type=F32)


def _bdot_tn(a, b):
    return lax.dot_general(a, b, (((1,), (1,)), ((0,), (0,))), preferred_element_type=F32)


def _bdot_bf(a, b):
    return _bdot(a.astype(BF16), b.astype(BF16))


def _bdot_01(a01_bf16, x):
    xh, xl = _split2(x)
    return _bdot(a01_bf16, xh) + _bdot(a01_bf16, xl)


def _dn_local_kernel(q_ref, k_ref, v_ref, g_ref, beta_ref, u_ref, w_ref, qg_ref, kd_ref, attn_ref, el_ref):
    c = DN_CHUNK
    dh = DN_HEAD_DIM
    i = lax.broadcasted_iota(jnp.int32, (c, c), 0)
    j = lax.broadcasted_iota(jnp.int32, (c, c), 1)
    eye = (i == j).astype(F32)

    nh = DN_HEADS
    nu = 2 * nh * DN_BATCH_CHUNKS
    rel = jnp.stack(([i - j] * nh + [j - i] * nh) * DN_BATCH_CHUNKS)
    incl = rel >= 0
    strict = rel > 0
    tri = incl.astype(BF16)

    def chunk(ci, carry):
        r0 = pl.multiple_of(ci * (c * DN_BATCH_CHUNKS), c * DN_BATCH_CHUNKS)
        rows_of = [pl.ds(r0 + m * c, c) for m in range(DN_BATCH_CHUNKS)]
        g_all = [g_ref[rows, :] for rows in rows_of]
        b_all = [beta_ref[rows, :] for rows in rows_of]
        g = jnp.stack([ga[:, n:n + 1] for ga in g_all for n in range(2 * nh)])
        beta = jnp.stack([ba[:, n:n + 1] for ba in b_all for n in range(2 * nh)])
        heads = lambda ref: jnp.stack([ref[rows, hd * dh:(hd + 1) * dh]
                                       for rows in rows_of for _ in range(2) for hd in range(nh)])
        q = heads(q_ref)
        k = heads(k_ref)
        v = heads(v_ref)
        gc = _bdot_01(tri, jnp.broadcast_to(g, (nu, c, dh)))
        dmat = _bdot_01(tri, jnp.where(strict, jnp.broadcast_to(g, (nu, c, c)), 0.0))
        decay = jnp.where(incl, jnp.exp(dmat), 0.0)
        kb = k * beta
        kbf = k.astype(BF16)
        a = jnp.where(strict, -(_bdot_nt(kb.astype(BF16), kbf) * decay), 0.0)
        tinv = eye + a
        p = a
        for _ in range(5):
            pb = p.astype(BF16)
            p = _bdot(pb, pb)
            tinv = tinv + _bdot_bf(tinv, p)
        egc = jnp.exp(gc)
        tb = tinv.astype(BF16)
        u = _bdot(tb, (v * beta).astype(BF16))
        w = _bdot(tb, (kb * egc).astype(BF16))
        attn = jnp.where(incl, _bdot_nt(q.astype(BF16), kbf) * decay, 0.0)
        ends = [c - 1 if (n // nh) % 2 == 0 else 0 for n in range(nu)]
        last = jnp.concatenate([gc[n:n + 1, e:e + 1, :] for n, e in enumerate(ends)], axis=0)
        kd = k * jnp.exp(last - gc)
        qg = q * egc
        el = jnp.exp(last)
        for n in range(nu):
            m, rest = divmod(n, 2 * nh)
            d, hd = divmod(rest, nh)
            rows = rows_of[m]
            sl = slice(hd * dh, (hd + 1) * dh)
            u_ref[d, rows, sl] = u[n]
            w_ref[d, rows, sl] = w[n].astype(w_ref.dtype)
            attn_ref[d, rows, hd * c:(hd + 1) * c] = attn[n].astype(attn_ref.dtype)
            kd_ref[d, rows, sl] = kd[n].astype(kd_ref.dtype)
            qg_ref[d, rows, sl] = qg[n].astype(qg_ref.dtype)
            el_ref[d, ci * DN_BATCH_CHUNKS + m, :, sl] = el[n]
        return carry

    lax.fori_loop(0, ROW_TILE // (c * DN_BATCH_CHUNKS), chunk, 0)


def _dn_local(q, k, v, g, beta):
    b, t, hw = q.shape
    cpt = ROW_TILE // DN_CHUNK
    rows = lambda wd: pl.BlockSpec((None, ROW_TILE, wd), lambda i, j: (i, j, 0))
    rows2 = lambda wd: pl.BlockSpec((2, None, ROW_TILE, wd), lambda i, j: (0, i, j, 0))
    aw = DN_HEADS * DN_CHUNK
    return pl.pallas_call(
        _dn_local_kernel,
        out_shape=[jax.ShapeDtypeStruct((2, b, t, hw), F32),
                   jax.ShapeDtypeStruct((2, b, t, hw), BF16),
                   jax.ShapeDtypeStruct((2, b, t, hw), BF16),
                   jax.ShapeDtypeStruct((2, b, t, hw), BF16),
                   jax.ShapeDtypeStruct((2, b, t, aw), BF16),
                   jax.ShapeDtypeStruct((2, b, t // DN_CHUNK, 1, hw), F32)],
        grid=(b, t // ROW_TILE),
        in_specs=[rows(hw), rows(hw), rows(hw), rows(g.shape[2]), rows(g.shape[2])],
        out_specs=[rows2(hw), rows2(hw), rows2(hw), rows2(hw), rows2(aw),
                   pl.BlockSpec((2, None, cpt, 1, hw), lambda i, j: (0, i, j, 0, 0))],
        compiler_params=_cparams("parallel", "parallel"),
        name="deltanet_local",
    )(q, k, v, g, beta)


def _dn_scan_kernel(*refs):
    ins = refs[:12]
    of_ref, ob_ref, state_ref = refs[12:]
    dh = DN_HEAD_DIM
    c = DN_CHUNK

    @pl.when(pl.program_id(1) == 0)
    def _():
        state_ref[...] = jnp.zeros_like(state_ref)

    nh = DN_HEADS

    def stacked(k, width):
        return jnp.stack([ins[6 * d + k][:, hd * width:(hd + 1) * width] for d in range(2) for hd in range(nh)])

    st = state_ref[...]
    stb = st.astype(BF16)
    v_new = stacked(0, dh) - _bdot(stacked(1, dh), stb)
    vnb = v_new.astype(BF16)
    o = _bdot(stacked(2, dh), stb) + _bdot(stacked(4, c), vnb)
    state_ref[...] = st * stacked(5, dh) + _bdot_tn(stacked(3, dh), vnb)
    for n in range(2 * nh):
        d, hd = divmod(n, nh)
        (of_ref, ob_ref)[d][:, hd * dh:(hd + 1) * dh] = o[n]


def _dn_scan(u, w, qg, kd, attn, el, ctx_len):
    _, b, t, hw = u.shape
    n = t // DN_CHUNK
    nc = ctx_len // DN_CHUNK
    aw = attn.shape[3]

    def specs(d):
        rows = lambda wd: pl.BlockSpec((None, None, DN_CHUNK, wd), lambda i, s: (d, i, _scan_chunk(d, s, nc, n), 0))
        return [rows(hw), rows(hw), rows(hw), rows(hw), rows(aw),
                pl.BlockSpec((None, None, None, 1, hw), lambda i, s: (d, i, _scan_chunk(d, s, nc, n), 0, 0))]

    out = lambda d: pl.BlockSpec((None, DN_CHUNK, hw), lambda i, s: (i, _scan_chunk(d, s, nc, n), 0))
    args = (u, w, qg, kd, attn, el)
    return pl.pallas_call(
        _dn_scan_kernel,
        out_shape=[jax.ShapeDtypeStruct((b, t, hw), F32)] * 2,
        grid=(b, n),
        in_specs=specs(0) + specs(1),
        out_specs=[out(0), out(1)],
        scratch_shapes=[pltpu.VMEM((2 * DN_HEADS, DN_HEAD_DIM, DN_HEAD_DIM), F32)],
        compiler_params=_cparams("parallel", "arbitrary"),
        name="deltanet_scan",
    )(*args, *args)


def _gated_norm_kernel(oa_ref, ob_ref, z_ref, gain_ref, out_ref):
    o = oa_ref[...] + ob_ref[...]
    z = z_ref[...].astype(F32)
    for j in range(0, o.shape[1], LANES):
        oh = _rms(o[:, j:j + LANES]) * gain_ref[...]
        out_ref[:, j:j + LANES] = (oh * _silu(z[:, j:j + LANES])).astype(out_ref.dtype)


def _gated_norm(o_fwd, o_bwd, z, gain_row):
    b, t, w = z.shape
    o_spec = lambda d: pl.BlockSpec((None, ROW_TILE, w), lambda i, j: (i, j, 0))
    return pl.pallas_call(
        _gated_norm_kernel,
        out_shape=jax.ShapeDtypeStruct((b, t, w), BF16),
        grid=(b, t // ROW_TILE),
        in_specs=[o_spec(0), o_spec(1),
                  pl.BlockSpec((None, ROW_TILE, w), lambda i, j: (i, j, 0)),
                  pl.BlockSpec((1, LANES), lambda i, j: (0, 0))],
        out_specs=pl.BlockSpec((None, ROW_TILE, w), lambda i, j: (i, j, 0)),
        compiler_params=_cparams("parallel", "parallel"),
        name="gated_norm",
    )(o_fwd, o_bwd, z, gain_row)


def _merge_kernel(h_ref, oa_ref, ob_ref, oc_ref, od_ref, wg_ref, wb_ref, wo_ref, x_ref, mod_ref, gains_ref,
                  wrh_ref, wrl_ref, br_ref, xn_ref, h2_ref, idx_ref, wgt_ref):
    d = x_ref.shape[1]
    subs = [pl.ds(n * MERGE_SUB, MERGE_SUB) for n in range(ROW_TILE // MERGE_SUB)]
    ys = []
    for rows in subs:
        h = h_ref[rows, :]
        merged = None
        for n, o_ref in enumerate((oa_ref, ob_ref, oc_ref, od_ref)):
            gate = jax.nn.sigmoid(_dot(h, wg_ref[:, n * d:(n + 1) * d]))
            term = gate * _dot(o_ref[rows, :], wb_ref[n])
            merged = term if merged is None else merged + term
        ys.append(_dot(merged.astype(BF16), wo_ref[...]))
    for rows, y in zip(subs, ys):
        x_new = x_ref[rows, :] + mod_ref[2:3, :] * (_rms(y) * gains_ref[1:2, :])
        xn_ref[rows, :] = x_new
        h2 = _rms(x_new) * gains_ref[2:3, :] * (1.0 + mod_ref[4:5, :]) + mod_ref[3:4, :]
        h2_ref[rows] = h2.reshape((MERGE_SUB,) + h2_ref.shape[1:])
        h2h, h2l = _split2(h2)
        logits = _dot(h2h, wrh_ref[...]) + _dot(h2l, wrh_ref[...]) + _dot(h2h, wrl_ref[...]) + br_ref[...]
        ne = logits.shape[1]
        lane = lax.broadcasted_iota(jnp.int32, logits.shape, 1)
        vals = []
        for kk in range(TOP_K):
            m = jnp.max(logits, axis=-1, keepdims=True)
            sel = jnp.min(jnp.where(logits == m, lane, ne), axis=-1, keepdims=True)
            sel = jnp.minimum(sel, ne - 1)
            idx_ref[rows, kk:kk + 1] = sel
            vals.append(m)
            logits = jnp.where(lane == sel, -jnp.inf, logits)
        es = [jnp.exp(vv - vals[0]) for vv in vals]
        tot = es[0] + es[1] + es[2] + es[3]
        for kk in range(TOP_K):
            wgt_ref[rows, kk:kk + 1] = es[kk] / tot


def _merge(h, branch_outs, wg, wb, wo, x, modtab, gains, w_router, b_router, nct, tile0):
    b, t, d = x.shape
    nt = t // ROW_TILE - tile0
    ne = w_router.shape[1]
    rows = lambda wd: pl.BlockSpec((None, ROW_TILE, wd), lambda i, j: (i, j + tile0, 0))
    bw = branch_outs[0].shape[2]
    wr_hi = w_router.astype(BF16)
    wr_lo = (w_router - wr_hi.astype(F32)).astype(BF16)
    return pl.pallas_call(
        _merge_kernel,
        out_shape=[jax.ShapeDtypeStruct((b, t, d), F32),
                   jax.ShapeDtypeStruct((b, t, d // LANES, LANES), F32),
                   jax.ShapeDtypeStruct((b, t, TOP_K), jnp.int32),
                   jax.ShapeDtypeStruct((b, t, TOP_K), F32)],
        grid=(b, nt),
        in_specs=[rows(d), rows(bw), rows(bw), rows(bw), rows(bw),
                  pl.BlockSpec(wg.shape, lambda i, j: (0, 0), pipeline_mode=pl.Buffered(1)),
                  pl.BlockSpec(wb.shape, lambda i, j: (0, 0, 0), pipeline_mode=pl.Buffered(1)),
                  pl.BlockSpec(wo.shape, lambda i, j: (0, 0), pipeline_mode=pl.Buffered(1)),
                  rows(d),
                  pl.BlockSpec((None, None, N_MOD, d), lambda i, j: (i, (j + tile0 >= nct).astype(jnp.int32), 0, 0)),
                  pl.BlockSpec(gains.shape, lambda i, j: (0, 0)),
                  pl.BlockSpec(w_router.shape, lambda i, j: (0, 0)),
                  pl.BlockSpec(w_router.shape, lambda i, j: (0, 0)),
                  pl.BlockSpec((1, ne), lambda i, j: (0, 0))],
        out_specs=[rows(d),
                   pl.BlockSpec((None, ROW_TILE, d // LANES, LANES), lambda i, j: (i, j + tile0, 0, 0)),
                   rows(TOP_K), rows(TOP_K)],
        compiler_params=_cparams("parallel", "parallel"),
        name="merge",
    )(h, *branch_outs, wg, wb, wo, x, modtab, gains, wr_hi, wr_lo, b_router.reshape(1, ne))


def _rank_kernel(idx_ref, rank_ref, count_ref, carry_ref, *, ne):
    @pl.when(pl.program_id(0) == 0)
    def _():
        carry_ref[...] = jnp.zeros_like(carry_ref)

    idx = idx_ref[...]
    tm = idx.shape[0]
    lane = lax.broadcasted_iota(jnp.int32, (tm, ne), 1)
    onehots = [(lane == idx[:, kk:kk + 1]) for kk in range(TOP_K)]
    member = onehots[0] | onehots[1] | onehots[2] | onehots[3]
    i = lax.broadcasted_iota(jnp.int32, (tm, tm), 0)
    j = lax.broadcasted_iota(jnp.int32, (tm, tm), 1)
    before = _dot((j < i).astype(BF16), member.astype(BF16)) + carry_ref[...]
    for kk in range(TOP_K):
        rk = jnp.sum(jnp.where(onehots[kk], before, 0.0), axis=-1, keepdims=True)
        rank_ref[:, kk:kk + 1] = rk.astype(jnp.int32)
    carry_ref[...] = carry_ref[...] + jnp.sum(member.astype(F32), axis=0, keepdims=True)
    count_ref[...] = carry_ref[...].astype(jnp.int32)


def _moe_rank(top_idx, ne):
    n = top_idx.shape[0]
    return pl.pallas_call(
        functools.partial(_rank_kernel, ne=ne),
        out_shape=[jax.ShapeDtypeStruct((n, TOP_K), jnp.int32), jax.ShapeDtypeStruct((1, ne), jnp.int32)],
        grid=(n // ROW_TILE,),
        in_specs=[pl.BlockSpec((ROW_TILE, TOP_K), lambda i: (i, 0))],
        out_specs=[pl.BlockSpec((ROW_TILE, TOP_K), lambda i: (i, 0)), pl.BlockSpec((1, ne), lambda i: (0, 0))],
        scratch_shapes=[pltpu.VMEM((1, ne), F32)],
        compiler_params=_cparams("arbitrary"),
        name="moe_rank",
    )(top_idx)


def _row_copy(src, dst, sem):
    return pltpu.make_async_copy(src, dst, sem)


def _dispatch_kernel(dest_ref, h_ref, buf_in, buf_hbm, sem):
    del buf_in

    def issue(r, carry):
        for kk in range(TOP_K):
            _row_copy(h_ref.at[r], buf_hbm.at[dest_ref[0, r * TOP_K + kk]], sem).start()
        return carry

    lax.fori_loop(0, ROW_TILE, issue, 0)
    _row_copy(buf_hbm.at[pl.ds(0, ROW_TILE * TOP_K)], buf_hbm.at[pl.ds(0, ROW_TILE * TOP_K)], sem).wait()


def _moe_dispatch(h2, dest, n_rows):
    n, s, l = h2.shape
    buf0 = jnp.zeros((n_rows, s, l), h2.dtype)
    dest2 = dest.reshape(n // ROW_TILE, 1, ROW_TILE * TOP_K)
    return pl.pallas_call(
        _dispatch_kernel,
        out_shape=jax.ShapeDtypeStruct((n_rows, s, l), h2.dtype),
        grid=(n // ROW_TILE,),
        in_specs=[pl.BlockSpec((None, 1, ROW_TILE * TOP_K), lambda i: (i, 0, 0), memory_space=pltpu.SMEM),
                  pl.BlockSpec((ROW_TILE, s, l), lambda i: (i, 0, 0)),
                  pl.BlockSpec(memory_space=pl.ANY)],
        out_specs=pl.BlockSpec(memory_space=pl.ANY),
        scratch_shapes=[pltpu.SemaphoreType.DMA],
        input_output_aliases={2: 0},
        compiler_params=pltpu.CompilerParams(dimension_semantics=("arbitrary",), vmem_limit_bytes=VMEM_LIMIT,
                                             has_side_effects=True),
        name="moe_dispatch",
    )(dest2, h2, buf0)


def _ffn_kernel(te_ref, nu_ref, x_ref, wgu_ref, bgu_ref, wd_ref, bd_ref, y_ref):
    i = pl.program_id(0)
    ns = x_ref.shape[1]

    @pl.when(i < nu_ref[0])
    def _():
        subs = [pl.ds(h * MOE_SUB, MOE_SUB) for h in range(MOE_TILE // MOE_SUB)]
        gus = []
        for rows in subs:
            x = x_ref[rows].reshape(MOE_SUB, ns * LANES).astype(BF16)
            gus.append(_dot(x, wgu_ref[...]) + bgu_ref[...])
        for rows, gu in zip(subs, gus):
            f = gu.shape[1] // 2
            gate = jnp.minimum(gu[:, :f], SWIGLU_LIMIT)
            up = jnp.clip(gu[:, f:], -SWIGLU_LIMIT, SWIGLU_LIMIT)
            act = (up + 1.0) * gate * jax.nn.sigmoid(SWIGLU_ALPHA * gate)
            y = _dot(act.astype(BF16), wd_ref[...]) + bd_ref[...]
            y_ref[rows] = y.reshape(MOE_SUB, ns, LANES)

    @pl.when(i >= nu_ref[0])
    def _():
        y_ref[...] = jnp.zeros_like(y_ref)


def _moe_ffn(buf, tile_expert, n_used, wgu, bgu, wd, bd):
    n_rows, s, l = buf.shape
    ne, d, f2 = wgu.shape
    grid_spec = pltpu.PrefetchScalarGridSpec(
        num_scalar_prefetch=2,
        grid=(n_rows // MOE_TILE,),
        in_specs=[pl.BlockSpec((MOE_TILE, s, l), lambda i, te, nu: (i, 0, 0)),
                  pl.BlockSpec((None, d, f2), lambda i, te, nu: (te[i], 0, 0)),
                  pl.BlockSpec((None, 1, f2), lambda i, te, nu: (te[i], 0, 0)),
                  pl.BlockSpec((None, f2 // 2, d), lambda i, te, nu: (te[i], 0, 0)),
                  pl.BlockSpec((None, 1, d), lambda i, te, nu: (te[i], 0, 0))],
        out_specs=pl.BlockSpec((MOE_TILE, s, l), lambda i, te, nu: (i, 0, 0)),
    )
    return pl.pallas_call(
        _ffn_kernel,
        out_shape=jax.ShapeDtypeStruct((n_rows, s, l), F32),
        grid_spec=grid_spec,
        compiler_params=_cparams("arbitrary"),
        name="moe_ffn",
    )(tile_expert, n_used, buf, wgu, bgu.reshape(ne, 1, f2), wd, bd.reshape(ne, 1, d))


def _combine_kernel(dest_ref, next_ref, wgt_ref, y_hbm, x_ref, mod_ref, gain_ref, o_ref, gat_ref, acc_ref, sem):
    step = pl.program_id(0) * pl.num_programs(1) + pl.program_id(1)
    n_steps = pl.num_programs(0) * pl.num_programs(1)
    slot = step % 2

    def gather(idx_ref, to_slot):
        def issue(r, carry):
            for kk in range(TOP_K):
                _row_copy(y_hbm.at[idx_ref[0, r * TOP_K + kk]], gat_ref.at[to_slot, r * TOP_K + kk],
                          sem.at[to_slot]).start()
            return carry

        lax.fori_loop(0, ROW_TILE, issue, 0)

    @pl.when(step == 0)
    def _():
        gather(dest_ref, slot)

    @pl.when(step + 1 < n_steps)
    def _():
        gather(next_ref, 1 - slot)

    _row_copy(y_hbm.at[pl.ds(0, ROW_TILE * TOP_K)], gat_ref.at[slot], sem.at[slot]).wait()

    def mix(r, carry):
        acc = None
        for kk in range(TOP_K):
            term = wgt_ref[0, r * TOP_K + kk] * gat_ref[slot, r * TOP_K + kk]
            acc = term if acc is None else acc + term
        acc_ref[r] = acc
        return carry

    lax.fori_loop(0, ROW_TILE, mix, 0)
    f = acc_ref[...].reshape(x_ref.shape)
    o_ref[...] = x_ref[...] + mod_ref[5:6, :] * (_rms(f) * gain_ref[...])


def _moe_combine(ybuf, dest, wgt, x, modtab, gain_row, nct, tile0):
    b, t, d = x.shape
    tpb = t // ROW_TILE
    nt = tpb - tile0
    s, l = ybuf.shape[1:]
    dest2 = dest.reshape(b * tpb, 1, ROW_TILE * TOP_K)
    wgt2 = wgt.reshape(b * tpb, 1, ROW_TILE * TOP_K)
    tile_of = lambda i, j: i * tpb + j + tile0

    def next_tile(i, j):
        last = j == nt - 1
        return jnp.minimum(tile_of(jnp.where(last, i + 1, i), jnp.where(last, 0, j + 1)), b * tpb - 1)

    smem_rows = lambda f: pl.BlockSpec((None, 1, ROW_TILE * TOP_K), lambda i, j: (f(i, j), 0, 0),
                                       memory_space=pltpu.SMEM)
    return pl.pallas_call(
        _combine_kernel,
        out_shape=jax.ShapeDtypeStruct((b, t, d), F32),
        grid=(b, nt),
        in_specs=[smem_rows(tile_of), smem_rows(next_tile), smem_rows(tile_of),
                  pl.BlockSpec(memory_space=pl.ANY),
                  pl.BlockSpec((None, ROW_TILE, d), lambda i, j: (i, j + tile0, 0)),
                  pl.BlockSpec((None, None, N_MOD, d), lambda i, j: (i, (j + tile0 >= nct).astype(jnp.int32), 0, 0)),
                  pl.BlockSpec((1, d), lambda i, j: (0, 0))],
        out_specs=pl.BlockSpec((None, ROW_TILE, d), lambda i, j: (i, j + tile0, 0)),
        scratch_shapes=[pltpu.VMEM((2, ROW_TILE * TOP_K, s, l), F32),
                        pltpu.VMEM((ROW_TILE, s, l), F32),
                        pltpu.SemaphoreType.DMA((2,))],
        compiler_params=_cparams("arbitrary", "arbitrary"),
        name="moe_combine",
    )(dest2, dest2, wgt2, ybuf, x, modtab, gain_row)


def _deinterleave(width):
    idx = np.arange(width).reshape(-1, HEAD_DIM // 2, 2)
    return np.concatenate([idx[:, :, 0], idx[:, :, 1]], axis=1).reshape(-1)


def _rope_tables(ang, ctx_len):
    cos = jnp.cos(ang)
    sin = jnp.sin(ang)
    cos64 = jnp.concatenate([cos, cos], axis=-1)
    sin64 = jnp.concatenate([-sin, sin], axis=-1)
    reps = LANES // HEAD_DIM
    cos_t = jnp.concatenate([jnp.ones((ctx_len, HEAD_DIM), F32), cos64], axis=0)
    sin_t = jnp.concatenate([jnp.zeros((ctx_len, HEAD_DIM), F32), sin64], axis=0)
    return jnp.tile(cos_t, (1, reps)), jnp.tile(sin_t, (1, reps))


def _axial_angles(n_tok):
    rows = n_tok // GRID_W
    row = jnp.repeat(jnp.arange(rows), GRID_W).astype(F32)
    col = jnp.tile(jnp.arange(GRID_W), rows).astype(F32)
    half = HEAD_DIM // 2
    inv = ROPE_THETA ** (-jnp.arange(0, half, 2, dtype=F32) / half)
    return jnp.concatenate([row[:, None] * inv, col[:, None] * inv], axis=-1)


def _line_angles(n_tok):
    pos = jnp.arange(n_tok, dtype=F32)
    inv = ROPE_THETA ** (-jnp.arange(0, RET_QK_DIM, 2, dtype=F32) / RET_QK_DIM)
    return pos[:, None] * inv


def _layer(x, modtab, p, layer_idx, ctx_len, last):
    b, t, d = x.shape
    nct = ctx_len // ROW_TILE
    del last
    tile0 = 0
    bwid = d // 2
    sizes = (bwid, GQA_KV_HEADS * HEAD_DIM, GQA_KV_HEADS * HEAD_DIM,
             bwid, bwid, bwid,
             RET_HEADS * RET_QK_DIM, RET_HEADS * RET_QK_DIM, bwid, bwid,
             3 * bwid, bwid, 2 * DN_HEADS, 2 * DN_HEADS,
             N_BRANCHES * d)
    cuts = np.concatenate([[0], np.cumsum(sizes)])
    w_in = p['w_in']
    col = lambda a, e: w_in[:, cuts[a]:cuts[e]]
    gains = p['norm_gain']
    ones_row = jnp.ones((1, LANES), F32)

    h = _modulate(x, modtab, gains[0:1], nct)

    seq = t - ctx_len
    cos_ax, sin_ax = _rope_tables(_axial_angles(seq), ctx_len)
    cos_ln, sin_ln = _rope_tables(_line_angles(seq), ctx_len)

    perm_q = _deinterleave(sizes[0])
    perm_k = _deinterleave(sizes[1])
    wa = jnp.concatenate([col(0, 1)[:, perm_q], col(1, 2)[:, perm_k], col(2, 3)], axis=1).astype(BF16)
    perm64 = _deinterleave(HEAD_DIM)
    qk_gain = p['gqa_qk_gain'].astype(F32)
    qscale = HEAD_DIM ** -0.5
    gain_a = jnp.concatenate([jnp.tile(qk_gain[0][perm64] * qscale, sizes[0] // HEAD_DIM),
                              jnp.tile(qk_gain[1][perm64], sizes[1] // HEAD_DIM)]).reshape(1, -1)
    qa, ka, va = _inproj(h, wa, cos_ax, sin_ax, gain_a,
                         [(sizes[0], BF16), (sizes[1], BF16), (sizes[2], BF16)], sizes[0] + sizes[1], True)
    oa = _gqa_attention(qa, ka, va, nct, ctx_len, tile0)

    perm_b = _deinterleave(bwid)
    wb_in = jnp.concatenate([col(3, 4)[:, perm_b] * qscale, col(4, 5)[:, perm_b], col(5, 6)], axis=1).astype(BF16)
    qb, kb, vb = _inproj(h, wb_in, cos_ax, sin_ax, ones_row,
                         [(bwid, BF16)] * 3, 2 * bwid, False)
    lam_init = 0.8 - 0.6 * math.exp(-0.3 * layer_idx)
    ob = _diff_attention(qb, kb, vb, p['diff_lambda'].astype(F32), p['diff_norm'].astype(F32).reshape(1, -1),
                         nct, ctx_len, tile0, lam_init)

    perm_c = _deinterleave(sizes[6])
    wc = jnp.concatenate([col(6, 7)[:, perm_c], col(7, 8)[:, perm_c] * (RET_QK_DIM ** -0.5), col(8, 10)],
                         axis=1).astype(BF16)
    qc, kc, vc, gc = _inproj(h, wc, cos_ln, sin_ln, ones_row,
                             [(sizes[6], BF16), (sizes[7], BF16), (bwid, BF16), (bwid, BF16)],
                             sizes[6] + sizes[7], False)
    log_gamma = jax.nn.log_sigmoid(p['ret_decay_logit'].astype(F32))
    oc_f, oc_b = _retention(qc, kc, vc, log_gamma, ctx_len)
    oc = _gated_norm(oc_f, oc_b, gc, ones_row)

    wd_in = col(10, 12).astype(BF16)
    w_ba = col(12, 14).astype(BF16)
    w_ba = jnp.pad(w_ba, ((0, 0), (0, LANES - w_ba.shape[1])))
    wd_all = jnp.concatenate([wd_in, w_ba], axis=1)
    qkv_d, z_d, ba_d = _inproj(h, wd_all, cos_ln, sin_ln, ones_row,
                               [(3 * bwid, F32), (bwid, BF16), (LANES, F32)], 0, False)
    nb = 2 * DN_HEADS
    rate_row = jnp.exp(p['dn_a_log'].astype(F32)).reshape(1, nb)
    bias_row = p['dn_dt_bias'].astype(F32).reshape(1, nb)
    qd, kd, vd, g_d, beta_d = _dn_prep(qkv_d, ba_d[:, :, :2 * nb], p['dn_conv_w'].astype(F32), rate_row, bias_row, nct)
    od_f, od_b = _dn_scan(*_dn_local(qd, kd, vd, g_d, beta_d), ctx_len)
    od = _gated_norm(od_f, od_b, z_d, p['dn_norm'].astype(F32).reshape(1, -1))

    wg = col(14, 15).astype(BF16)
    x_new, h2, top_idx, top_w = _merge(h, (oa, ob, oc, od), wg, p['w_branch'].astype(BF16),
                                       p['w_out'].astype(BF16), x, modtab, gains,
                                       p['w_router'].astype(F32), p['b_router'].astype(F32), nct, tile0)

    ne = p['w_router'].shape[1]
    n = b * t
    top_idx = top_idx.reshape(n, TOP_K)
    rank, counts = _moe_rank(top_idx, ne)
    counts = counts.reshape(ne)
    padded = (counts + MOE_TILE - 1) // MOE_TILE * MOE_TILE
    pad_ends = jnp.cumsum(padded)
    pad_starts = pad_ends - padded
    dest = pad_starts[top_idx] + rank
    n_tiles = (n * TOP_K) // MOE_TILE + ne
    tile_expert = jnp.minimum(jnp.searchsorted(pad_ends, jnp.arange(n_tiles) * MOE_TILE, side='right'),
                              ne - 1).astype(jnp.int32)
    n_used = (pad_ends[-1] // MOE_TILE).astype(jnp.int32).reshape(1)
    buf = _moe_dispatch(h2.reshape(n, d // LANES, LANES), dest, n_tiles * MOE_TILE)
    ybuf = _moe_ffn(buf, tile_expert, n_used, p['w_gate_up'].astype(BF16), p['b_gate_up'].astype(F32),
                    p['w_down'].astype(BF16), p['b_down'].astype(F32))
    return _moe_combine(ybuf, dest, top_w.reshape(n, TOP_K), x_new, modtab, gains[3:4], nct, tile0)


def kernel(x, c, ctx, c_ctx, w_mod, b_mod, norm_gain, w_in, gqa_qk_gain, diff_lambda, diff_norm, ret_decay_logit,
           dn_conv_w, dn_a_log, dn_dt_bias, dn_norm, w_branch, w_out, w_router, b_router, w_gate_up, b_gate_up,
           w_down, b_down):
    b, seq, d = x.shape
    ctx_len = ctx.shape[1]
    depth = w_mod.shape[0]
    assert ctx_len % ROW_TILE == 0 and seq % ROW_TILE == 0 and seq % GRID_W == 0 and d % LANES == 0
    xa = jnp.concatenate([ctx, x], axis=1).astype(F32)
    c_all = jnp.concatenate([c, c_ctx[None, :]], axis=0).astype(F32)
    for l in range(depth):
        p = {
            'norm_gain': norm_gain[l].astype(F32), 'w_in': w_in[l], 'gqa_qk_gain': gqa_qk_gain[l],
            'diff_lambda': diff_lambda[l], 'diff_norm': diff_norm[l], 'ret_decay_logit': ret_decay_logit[l],
            'dn_conv_w': dn_conv_w[l], 'dn_a_log': dn_a_log[l], 'dn_dt_bias': dn_dt_bias[l], 'dn_norm': dn_norm[l],
            'w_branch': w_branch[l], 'w_out': w_out[l], 'w_router': w_router[l], 'b_router': b_router[l],
            'w_gate_up': w_gate_up[l], 'b_gate_up': b_gate_up[l], 'w_down': w_down[l], 'b_down': b_down[l],
        }
        mod = _mod_table(c_all, w_mod[l].astype(F32), b_mod[l].astype(F32))
        mod = mod.reshape(b + 1, N_MOD, d)
        modtab = jnp.stack([jnp.broadcast_to(mod[b], (b, N_MOD, d)), mod[:b]], axis=1)
        xa = _layer(xa, modtab, p, l, ctx_len, l == depth - 1)
    return xa[:, ctx_len:, :]
```

```python
import functools
import math

import numpy as np
import jax
import jax.numpy as jnp
from jax import lax
from jax.experimental import pallas as pl
from jax.experimental.pallas import tpu as pltpu

F32 = jnp.float32
BF16 = jnp.bfloat16

GRID_W = 64
RMS_EPS = 1e-6
ROPE_THETA = 10000.0
HEAD_DIM = 64
GQA_KV_HEADS = 2
GQA_GROUP = 4
DIFF_HEADS = 4
RET_HEADS = 4
RET_QK_DIM = 64
RET_V_DIM = 128
DN_HEADS = 4
DN_HEAD_DIM = 128
DN_CONV = 5
DN_CHUNK = 64
TOP_K = 4
SWIGLU_LIMIT = 7.0
SWIGLU_ALPHA = 1.702
N_MOD = 6
N_BRANCHES = 4

LANES = 128
SUBLANES = 8
ROW_TILE = 256
RET_CHUNK = 128
MERGE_SUB = 128
ATT_KEY_BLOCK = 256
ATT_SKEW = 3
DN_BATCH_CHUNKS = 4
MOE_TILE = 512
MOE_SUB = 256
VMEM_LIMIT = 56 * 1024 * 1024


def _cparams(*sem):
    return pltpu.CompilerParams(dimension_semantics=sem, vmem_limit_bytes=VMEM_LIMIT)


def _const_spec(shape):
    nd = len(shape)
    return pl.BlockSpec(shape, lambda *_: (0,) * nd)


def _rms(x):
    return x * lax.rsqrt(jnp.mean(x * x, axis=-1, keepdims=True) + RMS_EPS)


def _silu(x):
    return x * jax.nn.sigmoid(x)


def _dot(a, b):
    return jnp.dot(a, b, preferred_element_type=F32)


def _dot_nt(a, b):
    return lax.dot_general(a, b, (((1,), (1,)), ((), ())), preferred_element_type=F32)


def _dot_tn(a, b):
    return lax.dot_general(a, b, (((0,), (0,)), ((), ())), preferred_element_type=F32)


def _mod_kernel(c_ref, w_ref, b_ref, o_ref):
    o_ref[...] = _dot(_silu(c_ref[...]), w_ref[...]) + b_ref[...]


def _mod_table(c_all, w_mod, b_mod):
    m, d = c_all.shape
    n = w_mod.shape[1]
    tn = d
    return pl.pallas_call(
        _mod_kernel,
        out_shape=jax.ShapeDtypeStruct((m, n), F32),
        grid=(n // tn,),
        in_specs=[pl.BlockSpec((m, d), lambda j: (0, 0)),
                  pl.BlockSpec((d, tn), lambda j: (0, j)),
                  pl.BlockSpec((1, tn), lambda j: (0, j))],
        out_specs=pl.BlockSpec((m, tn), lambda j: (0, j)),
        compiler_params=_cparams("parallel"),
        name="mod_table",
    )(c_all, w_mod, b_mod.reshape(1, n))


def _modulate_kernel(x_ref, mod_ref, gain_ref, h_ref):
    xn = _rms(x_ref[...]) * gain_ref[...]
    h = xn * (1.0 + mod_ref[1:2, :]) + mod_ref[0:1, :]
    h_ref[...] = h.astype(h_ref.dtype)


def _modulate(x, modtab, gain_row, nct):
    b, t, d = x.shape
    return pl.pallas_call(
        _modulate_kernel,
        out_shape=jax.ShapeDtypeStruct((b, t, d), BF16),
        grid=(b, t // ROW_TILE),
        in_specs=[pl.BlockSpec((None, ROW_TILE, d), lambda i, j: (i, j, 0)),
                  pl.BlockSpec((None, None, N_MOD, d), lambda i, j: (i, (j >= nct).astype(jnp.int32), 0, 0)),
                  pl.BlockSpec((1, d), lambda i, j: (0, 0))],
        out_specs=pl.BlockSpec((None, ROW_TILE, d), lambda i, j: (i, j, 0)),
        compiler_params=_cparams("parallel", "parallel"),
        name="modulate",
    )(x, modtab, gain_row)


def _swap_halves(y):
    lane = lax.broadcasted_iota(jnp.int32, y.shape, 1)
    first = (lane % HEAD_DIM) < (HEAD_DIM // 2)
    return jnp.where(first, pltpu.roll(y, LANES - HEAD_DIM // 2, 1), pltpu.roll(y, HEAD_DIM // 2, 1))


def _inproj_kernel(h_ref, w_ref, cos_ref, sin_ref, gain_ref, seg_ref, *out_refs, widths, n_rope, do_norm):
    h = h_ref[...]
    cos = cos_ref[...]
    sin = sin_ref[...]
    col = 0
    for o_ref, wd in zip(out_refs, widths):
        u = _dot(h, w_ref[:, col:col + wd])
        for j in range(0, wd, LANES):
            wj = min(LANES, wd - j)
            y = u[:, j: j + wj]
            if col + j < n_rope:
                if do_norm:
                    ms = _dot((y * y).astype(BF16), seg_ref[...])
                    y = y * lax.rsqrt(ms + RMS_EPS) * gain_ref[:, col + j: col + j + wj]
                y = y * cos + _swap_halves(y) * sin
            o_ref[:, j: j + wj] = y.astype(o_ref.dtype)
        col += wd


def _inproj(h, w, cos, sin, gain_row, outs, n_rope, do_norm):
    b, t, d = h.shape
    c = w.shape[1]
    widths = tuple(o[0] for o in outs)
    assert sum(widths) == c
    seg = np.kron(np.eye(LANES // HEAD_DIM), np.full((HEAD_DIM, HEAD_DIM), 1.0 / HEAD_DIM))
    seg = jnp.asarray(seg, BF16)
    kern = functools.partial(_inproj_kernel, widths=widths, n_rope=n_rope, do_norm=do_norm)
    return pl.pallas_call(
        kern,
        out_shape=[jax.ShapeDtypeStruct((b, t, wd), dt) for wd, dt in outs],
        grid=(b, t // ROW_TILE),
        in_specs=[pl.BlockSpec((None, ROW_TILE, d), lambda i, j: (i, j, 0)),
                  pl.BlockSpec((d, c), lambda i, j: (0, 0)),
                  pl.BlockSpec((ROW_TILE, LANES), lambda i, j: (j, 0)),
                  pl.BlockSpec((ROW_TILE, LANES), lambda i, j: (j, 0)),
                  pl.BlockSpec(gain_row.shape, lambda i, j: (0, 0)),
                  pl.BlockSpec((LANES, LANES), lambda i, j: (0, 0))],
        out_specs=[pl.BlockSpec((None, ROW_TILE, wd), lambda i, j: (i, j, 0)) for wd, _ in outs],
        compiler_params=_cparams("parallel", "parallel"),
        name="inproj",
    )(h, w, cos, sin, gain_row, seg)


def _softmax_attend(n_maps, nk, query, keys, values, emit, s_ref):
    kb = min(ATT_KEY_BLOCK, nk)
    blocks = [slice(j * kb, (j + 1) * kb) for j in range(nk // kb)]
    rows = s_ref.shape[1]
    lane_pieces = lambda a: [a[:, c:c + LANES] for c in range(0, kb, LANES)]

    def pass1_block(n, q, blk, m_run):
        s = _dot_nt(q, keys(n, blk))
        s_ref[n % 2, :, blk] = s
        for piece in lane_pieces(s):
            m_run = jnp.maximum(m_run, piece)
        return m_run

    neg = jnp.full((rows, LANES), -jnp.inf, F32)
    q_next = query(0)
    m_run = neg
    for blk in blocks:
        m_run = pass1_block(0, q_next, blk, m_run)
    for n in range(n_maps):
        m = jnp.max(m_run, axis=-1, keepdims=True)
        m_run = neg
        if n + 1 < n_maps:
            q_next = query(n + 1)
        l_run = jnp.zeros((rows, LANES), F32)
        acc = None
        ahead = blocks[:ATT_SKEW] if n + 1 < n_maps else []
        for blk in ahead:
            m_run = pass1_block(n + 1, q_next, blk, m_run)
        for idx, blk in enumerate(blocks):
            if n + 1 < n_maps and idx + ATT_SKEW < len(blocks):
                m_run = pass1_block(n + 1, q_next, blocks[idx + ATT_SKEW], m_run)
            p = jnp.exp(s_ref[n % 2, :, blk] - m)
            for piece in lane_pieces(p):
                l_run = l_run + piece
            pv = _dot(p.astype(BF16), values(n, blk))
            acc = pv if acc is None else acc + pv
        emit(n, acc / jnp.sum(l_run, axis=-1, keepdims=True))


def _gqa_kernel(q_ref, k_ref, v_ref, o_ref, s_ref, *, nct, ctx_len, tile0):
    t = pl.program_id(1) + tile0

    def run(nk):
        head = lambda n: slice(n * HEAD_DIM, (n + 1) * HEAD_DIM)
        kv_head = lambda n: head(n // GQA_GROUP)

        def emit(n, o):
            o_ref[:, head(n)] = o.astype(o_ref.dtype)

        _softmax_attend(GQA_KV_HEADS * GQA_GROUP, nk,
                        lambda n: q_ref[:, head(n)],
                        lambda n, blk: k_ref[blk, kv_head(n)],
                        lambda n, blk: v_ref[blk, kv_head(n)],
                        emit, s_ref)

    if tile0 < nct:
        @pl.when(t < nct)
        def _():
            run(ctx_len)

    @pl.when(t >= nct)
    def _():
        run(k_ref.shape[0])


def _gqa_attention(q, k, v, nct, ctx_len, tile0):
    b, t, cq = q.shape
    nt = t // ROW_TILE - tile0
    kern = functools.partial(_gqa_kernel, nct=nct, ctx_len=ctx_len, tile0=tile0)
    return pl.pallas_call(
        kern,
        out_shape=jax.ShapeDtypeStruct((b, t, cq), BF16),
        grid=(b, nt),
        in_specs=[pl.BlockSpec((None, ROW_TILE, cq), lambda i, j: (i, j + tile0, 0)),
                  pl.BlockSpec((None, t, k.shape[2]), lambda i, j: (i, 0, 0)),
                  pl.BlockSpec((None, t, v.shape[2]), lambda i, j: (i, 0, 0))],
        out_specs=pl.BlockSpec((None, ROW_TILE, cq), lambda i, j: (i, j + tile0, 0)),
        scratch_shapes=[pltpu.VMEM((2, ROW_TILE, t), F32)],
        compiler_params=_cparams("parallel", "parallel"),
        name="gqa_attention",
    )(q, k, v)


def _diff_kernel(q_ref, k_ref, v_ref, lam_ref, gain_ref, o_ref, s_ref, *, nct, ctx_len, tile0, lam_init):
    t = pl.program_id(1) + tile0
    lp = lam_ref[...]
    lam = (jnp.exp(jnp.sum(lp[0:1, :] * lp[1:2, :], axis=-1, keepdims=True))
           - jnp.exp(jnp.sum(lp[2:3, :] * lp[3:4, :], axis=-1, keepdims=True)) + lam_init)
    dv = 2 * HEAD_DIM

    def run(nk):
        qk_cols = lambda n: slice(n * HEAD_DIM, (n + 1) * HEAD_DIM)
        v_cols = lambda n: slice((n // 2) * dv, (n // 2 + 1) * dv)
        first = {}

        def emit(n, o):
            if n % 2 == 0:
                first[n // 2] = o
            else:
                out = _rms(first.pop(n // 2) - lam * o) * gain_ref[...] * (1.0 - lam_init)
                o_ref[:, v_cols(n)] = out.astype(o_ref.dtype)

        _softmax_attend(2 * DIFF_HEADS, nk,
                        lambda n: q_ref[:, qk_cols(n)],
                        lambda n, blk: k_ref[blk, qk_cols(n)],
                        lambda n, blk: v_ref[blk, v_cols(n)],
                        emit, s_ref)

    if tile0 < nct:
        @pl.when(t < nct)
        def _():
            run(ctx_len)

    @pl.when(t >= nct)
    def _():
        run(k_ref.shape[0])


def _diff_attention(q, k, v, lam_params, gain_row, nct, ctx_len, tile0, lam_init):
    b, t, cq = q.shape
    nt = t // ROW_TILE - tile0
    kern = functools.partial(_diff_kernel, nct=nct, ctx_len=ctx_len, tile0=tile0, lam_init=lam_init)
    return pl.pallas_call(
        kern,
        out_shape=jax.ShapeDtypeStruct((b, t, cq), BF16),
        grid=(b, nt),
        in_specs=[pl.BlockSpec((None, ROW_TILE, cq), lambda i, j: (i, j + tile0, 0)),
                  pl.BlockSpec((None, t, cq), lambda i, j: (i, 0, 0)),
                  pl.BlockSpec((None, t, cq), lambda i, j: (i, 0, 0)),
                  pl.BlockSpec(lam_params.shape, lambda i, j: (0, 0)),
                  pl.BlockSpec(gain_row.shape, lambda i, j: (0, 0))],
        out_specs=pl.BlockSpec((None, ROW_TILE, cq), lambda i, j: (i, j + tile0, 0)),
        scratch_shapes=[pltpu.VMEM((2, ROW_TILE, t), F32)],
        compiler_params=_cparams("parallel", "parallel"),
        name="diff_attention",
    )(q, k, v, lam_params, gain_row)


def _scan_chunk(d, s, n_ctx_chunks, n_chunks):
    rev = jnp.where(s < n_ctx_chunks, n_ctx_chunks - 1 - s, n_chunks + n_ctx_chunks - 1 - s)
    return jnp.where(d == 0, s, rev)


def _ret_kernel(lg_ref, qf_ref, kf_ref, vf_ref, qb_ref, kb_ref, vb_ref, of_ref, ob_ref, state_ref):
    c = RET_CHUNK
    nh = RET_HEADS

    @pl.when(pl.program_id(1) == 0)
    def _():
        state_ref[...] = jnp.zeros_like(state_ref)

    i = lax.broadcasted_iota(jnp.int32, (c, c), 0)
    j = lax.broadcasted_iota(jnp.int32, (c, c), 1)
    r = lax.broadcasted_iota(jnp.int32, (c, 1), 0)
    lg = jnp.stack([jnp.full((1, 1), lg_ref[d, hd], F32) for d in range(2) for hd in range(nh)])
    rel = jnp.stack([i - j] * nh + [j - i] * nh)
    intra = jnp.where(rel >= 0, jnp.exp(lg * jnp.maximum(rel, 0).astype(F32)), 0.0)
    pos_q = jnp.stack([r + 1] * nh + [c - r] * nh).astype(F32)
    pos_k = jnp.stack([c - 1 - r] * nh + [r] * nh).astype(F32)

    def stacked(refs, width):
        return jnp.stack([ref[:, hd * width:(hd + 1) * width] for ref in refs for hd in range(nh)])

    q = stacked((qf_ref, qb_ref), RET_QK_DIM)
    k = stacked((kf_ref, kb_ref), RET_QK_DIM)
    v = stacked((vf_ref, vb_ref), RET_V_DIM)
    st = state_ref[...]
    scores = _bdot_nt(q, k) * intra
    qd = (q.astype(F32) * jnp.exp(lg * pos_q)).astype(BF16)
    o = _bdot(scores.astype(BF16), v) + _bdot(qd, st.astype(BF16))
    kd = (k.astype(F32) * jnp.exp(lg * pos_k)).astype(BF16)
    state_ref[...] = st * jnp.exp(lg * c) + _bdot_tn(kd, v)
    for n in range(2 * nh):
        d, hd = divmod(n, nh)
        (of_ref, ob_ref)[d][:, hd * RET_V_DIM:(hd + 1) * RET_V_DIM] = o[n]


def _retention(q, k, v, log_gamma, ctx_len):
    b, t, _ = q.shape
    n = t // RET_CHUNK
    nc = ctx_len // RET_CHUNK

    def specs(d):
        rows = lambda a: pl.BlockSpec((None, RET_CHUNK, a.shape[2]), lambda i, s: (i, _scan_chunk(d, s, nc, n), 0))
        return [rows(q), rows(k), rows(v)]

    return pl.pallas_call(
        _ret_kernel,
        out_shape=[jax.ShapeDtypeStruct((b, t, v.shape[2]), F32)] * 2,
        grid=(b, n),
        in_specs=[pl.BlockSpec(memory_space=pltpu.SMEM)] + specs(0) + specs(1),
        out_specs=[specs(0)[2], specs(1)[2]],
        scratch_shapes=[pltpu.VMEM((2 * RET_HEADS, RET_QK_DIM, RET_V_DIM), F32)],
        compiler_params=_cparams("parallel", "arbitrary"),
        name="retention_scan",
    )(log_gamma, q, k, v, q, k, v)


def _dn_prep_kernel(prev_ref, cur_ref, next_ref, cw_ref, ba_ref, rate_ref, bias_ref,
                    q_ref, k_ref, v_ref, g_ref, beta_ref, *, nct, n_tiles):
    t = pl.program_id(1)
    pad = DN_CONV // 2
    first = (t == 0) | (t == nct)
    last = (t == nct - 1) | (t == n_tiles - 1)
    prev = jnp.where(first, 0.0, prev_ref[...])
    nxt = jnp.where(last, 0.0, next_ref[...])
    ext = jnp.concatenate([prev, cur_ref[...], nxt], axis=0)
    rows = ext.shape[0]
    acc = None
    for tap in range(DN_CONV):
        sh = (pad - tap) % rows
        x = ext if sh == 0 else pltpu.roll(ext, sh, 0)
        term = x[SUBLANES:SUBLANES + ROW_TILE, :] * cw_ref[tap:tap + 1, :]
        acc = term if acc is None else acc + term
    y = _silu(acc)
    hw = DN_HEADS * DN_HEAD_DIM
    for hd in range(DN_HEADS):
        sl = slice(hd * DN_HEAD_DIM, (hd + 1) * DN_HEAD_DIM)
        qh = y[:, sl]
        kh = y[:, hw + hd * DN_HEAD_DIM: hw + (hd + 1) * DN_HEAD_DIM]
        q_ref[:, sl] = qh * lax.rsqrt(jnp.sum(qh * qh, axis=-1, keepdims=True) + RMS_EPS) * (DN_HEAD_DIM ** -0.5)
        k_ref[:, sl] = kh * lax.rsqrt(jnp.sum(kh * kh, axis=-1, keepdims=True) + RMS_EPS)
    v_ref[...] = y[:, 2 * hw:3 * hw]
    nb = 2 * DN_HEADS
    ba = ba_ref[...]
    beta_ref[...] = jax.nn.sigmoid(ba[:, 0:nb])
    a = ba[:, nb:2 * nb] + bias_ref[...]
    softplus = jnp.maximum(a, 0.0) + jnp.log(1.0 + jnp.exp(-jnp.abs(a)))
    g_ref[...] = -rate_ref[...] * softplus


def _dn_prep(qkv, ba, conv_w, rate_row, bias_row, nct):
    b, t, c = qkv.shape
    n_tiles = t // ROW_TILE
    hb = ROW_TILE // SUBLANES
    last_blk = t // SUBLANES - 1
    hw = DN_HEADS * DN_HEAD_DIM
    nb = 2 * DN_HEADS
    kern = functools.partial(_dn_prep_kernel, nct=nct, n_tiles=n_tiles)
    row_spec = lambda wd: pl.BlockSpec((None, ROW_TILE, wd), lambda i, j: (i, j, 0))
    return pl.pallas_call(
        kern,
        out_shape=[jax.ShapeDtypeStruct((b, t, hw), F32)] * 3 + [jax.ShapeDtypeStruct((b, t, nb), F32)] * 2,
        grid=(b, n_tiles),
        in_specs=[pl.BlockSpec((None, SUBLANES, c), lambda i, j: (i, jnp.maximum(j * hb - 1, 0), 0)),
                  row_spec(c),
                  pl.BlockSpec((None, SUBLANES, c), lambda i, j: (i, jnp.minimum((j + 1) * hb, last_blk), 0)),
                  pl.BlockSpec(conv_w.shape, lambda i, j: (0, 0)),
                  row_spec(2 * nb),
                  pl.BlockSpec((1, nb), lambda i, j: (0, 0)),
                  pl.BlockSpec((1, nb), lambda i, j: (0, 0))],
        out_specs=[row_spec(hw)] * 3 + [row_spec(nb)] * 2,
        compiler_params=_cparams("parallel", "parallel"),
        name="deltanet_prep",
    )(qkv, qkv, qkv, conv_w, ba, rate_row, bias_row)


def _split2(x):
    hi = x.astype(BF16)
    lo = (x - hi.astype(F32)).astype(BF16)
    return hi, lo


def _bdot(a, b):
    return lax.dot_general(a, b, (((2,), (1,)), ((0,), (0,))), preferred_element_type=F32)


def _bdot_nt(a, b):
    return lax.dot_general(a, b, (((2,), (2,)), ((0,), (0,))), preferred_element_type=F32)


def _bdot_tn(a, b):
    return lax.dot_general(a, b, (((1,), (1,)), ((0,), (0,))), preferred_element_type=F32)


def _bdot_bf(a, b):
    return _bdot(a.astype(BF16), b.astype(BF16))


def _bdot_01(a01_bf16, x):
    xh, xl = _split2(x)
    return _bdot(a01_bf16, xh) + _bdot(a01_bf16, xl)


def _dn_local_kernel(q_ref, k_ref, v_ref, g_ref, beta_ref, u_ref, w_ref, qg_ref, kd_ref, attn_ref, el_ref):
    c = DN_CHUNK
    dh = DN_HEAD_DIM
    i = lax.broadcasted_iota(jnp.int32, (c, c), 0)
    j = lax.broadcasted_iota(jnp.int32, (c, c), 1)
    eye = (i == j).astype(F32)

    nh = DN_HEADS
    nu = 2 * nh * DN_BATCH_CHUNKS
    rel = jnp.stack(([i - j] * nh + [j - i] * nh) * DN_BATCH_CHUNKS)
    incl = rel >= 0
    strict = rel > 0
    tri = incl.astype(BF16)

    def chunk(ci, carry):
        r0 = pl.multiple_of(ci * (c * DN_BATCH_CHUNKS), c * DN_BATCH_CHUNKS)
        rows_of = [pl.ds(r0 + m * c, c) for m in range(DN_BATCH_CHUNKS)]
        g_all = [g_ref[rows, :] for rows in rows_of]
        b_all = [beta_ref[rows, :] for rows in rows_of]
        g = jnp.stack([ga[:, n:n + 1] for ga in g_all for n in range(2 * nh)])
        beta = jnp.stack([ba[:, n:n + 1] for ba in b_all for n in range(2 * nh)])
        heads = lambda ref: jnp.stack([ref[rows, hd * dh:(hd + 1) * dh]
                                       for rows in rows_of for _ in range(2) for hd in range(nh)])
        q = heads(q_ref)
        k = heads(k_ref)
        v = heads(v_ref)
        gc = _bdot_01(tri, jnp.broadcast_to(g, (nu, c, dh)))
        dmat = _bdot_01(tri, jnp.where(strict, jnp.broadcast_to(g, (nu, c, c)), 0.0))
        decay = jnp.where(incl, jnp.exp(dmat), 0.0)
        kb = k * beta
        kbf = k.astype(BF16)
        a = jnp.where(strict, -(_bdot_nt(kb.astype(BF16), kbf) * decay), 0.0)
        tinv = eye + a
        p = a
        for _ in range(5):
            pb = p.astype(BF16)
            p = _bdot(pb, pb)
            tinv = tinv + _bdot_bf(tinv, p)
        egc = jnp.exp(gc)
        tb = tinv.astype(BF16)
        u = _bdot(tb, (v * beta).astype(BF16))
        w = _bdot(tb, (kb * egc).astype(BF16))
        attn = jnp.where(incl, _bdot_nt(q.astype(BF16), kbf) * decay, 0.0)
        ends = [c - 1 if (n // nh) % 2 == 0 else 0 for n in range(nu)]
        last = jnp.concatenate([gc[n:n + 1, e:e + 1, :] for n, e in enumerate(ends)], axis=0)
        kd = k * jnp.exp(last - gc)
        qg = q * egc
        el = jnp.exp(last)
        for n in range(nu):
            m, rest = divmod(n, 2 * nh)
            d, hd = divmod(rest, nh)
            rows = rows_of[m]
            sl = slice(hd * dh, (hd + 1) * dh)
            u_ref[d, rows, sl] = u[n]
            w_ref[d, rows, sl] = w[n].astype(w_ref.dtype)
            attn_ref[d, rows, hd * c:(hd + 1) * c] = attn[n].astype(attn_ref.dtype)
            kd_ref[d, rows, sl] = kd[n].astype(kd_ref.dtype)
            qg_ref[d, rows, sl] = qg[n].astype(qg_ref.dtype)
            el_ref[d, ci * DN_BATCH_CHUNKS + m, :, sl] = el[n]
        return carry

    lax.fori_loop(0, ROW_TILE // (c * DN_BATCH_CHUNKS), chunk, 0)


def _dn_local(q, k, v, g, beta):
    b, t, hw = q.shape
    cpt = ROW_TILE // DN_CHUNK
    rows = lambda wd: pl.BlockSpec((None, ROW_TILE, wd), lambda i, j: (i, j, 0))
    rows2 = lambda wd: pl.BlockSpec((2, None, ROW_TILE, wd), lambda i, j: (0, i, j, 0))
    aw = DN_HEADS * DN_CHUNK
    return pl.pallas_call(
        _dn_local_kernel,
        out_shape=[jax.ShapeDtypeStruct((2, b, t, hw), F32),
                   jax.ShapeDtypeStruct((2, b, t, hw), BF16),
                   jax.ShapeDtypeStruct((2, b, t, hw), BF16),
                   jax.ShapeDtypeStruct((2, b, t, hw), BF16),
                   jax.ShapeDtypeStruct((2, b, t, aw), BF16),
                   jax.ShapeDtypeStruct((2, b, t // DN_CHUNK, 1, hw), F32)],
        grid=(b, t // ROW_TILE),
        in_specs=[rows(hw), rows(hw), rows(hw), rows(g.shape[2]), rows(g.shape[2])],
        out_specs=[rows2(hw), rows2(hw), rows2(hw), rows2(hw), rows2(aw),
                   pl.BlockSpec((2, None, cpt, 1, hw), lambda i, j: (0, i, j, 0, 0))],
        compiler_params=_cparams("parallel", "parallel"),
        name="deltanet_local",
    )(q, k, v, g, beta)


def _dn_scan_kernel(*refs):
    ins = refs[:12]
    of_ref, ob_ref, state_ref = refs[12:]
    dh = DN_HEAD_DIM
    c = DN_CHUNK

    @pl.when(pl.program_id(1) == 0)
    def _():
        state_ref[...] = jnp.zeros_like(state_ref)

    nh = DN_HEADS

    def stacked(k, width):
        return jnp.stack([ins[6 * d + k][:, hd * width:(hd + 1) * width] for d in range(2) for hd in range(nh)])

    st = state_ref[...]
    stb = st.astype(BF16)
    v_new = stacked(0, dh) - _bdot(stacked(1, dh), stb)
    vnb = v_new.astype(BF16)
    o = _bdot(stacked(2, dh), stb) + _bdot(stacked(4, c), vnb)
    state_ref[...] = st * stacked(5, dh) + _bdot_tn(stacked(3, dh), vnb)
    for n in range(2 * nh):
        d, hd = divmod(n, nh)
        (of_ref, ob_ref)[d][:, hd * dh:(hd + 1) * dh] = o[n]


def _dn_scan(u, w, qg, kd, attn, el, ctx_len):
    _, b, t, hw = u.shape
    n = t // DN_CHUNK
    nc = ctx_len // DN_CHUNK
    aw = attn.shape[3]

    def specs(d):
        rows = lambda wd: pl.BlockSpec((None, None, DN_CHUNK, wd), lambda i, s: (d, i, _scan_chunk(d, s, nc, n), 0))
        return [rows(hw), rows(hw), rows(hw), rows(hw), rows(aw),
                pl.BlockSpec((None, None, None, 1, hw), lambda i, s: (d, i, _scan_chunk(d, s, nc, n), 0, 0))]

    out = lambda d: pl.BlockSpec((None, DN_CHUNK, hw), lambda i, s: (i, _scan_chunk(d, s, nc, n), 0))
    args = (u, w, qg, kd, attn, el)
    return pl.pallas_call(
        _dn_scan_kernel,
        out_shape=[jax.ShapeDtypeStruct((b, t, hw), F32)] * 2,
        grid=(b, n),
        in_specs=specs(0) + specs(1),
        out_specs=[out(0), out(1)],
        scratch_shapes=[pltpu.VMEM((2 * DN_HEADS, DN_HEAD_DIM, DN_HEAD_DIM), F32)],
        compiler_params=_cparams("parallel", "arbitrary"),
        name="deltanet_scan",
    )(*args, *args)


def _gated_norm_kernel(oa_ref, ob_ref, z_ref, gain_ref, out_ref):
    o = oa_ref[...] + ob_ref[...]
    z = z_ref[...].astype(F32)
    for j in range(0, o.shape[1], LANES):
        oh = _rms(o[:, j:j + LANES]) * gain_ref[...]
        out_ref[:, j:j + LANES] = (oh * _silu(z[:, j:j + LANES])).astype(out_ref.dtype)


def _gated_norm(o_fwd, o_bwd, z, gain_row):
    b, t, w = z.shape
    o_spec = lambda d: pl.BlockSpec((None, ROW_TILE, w), lambda i, j: (i, j, 0))
    return pl.pallas_call(
        _gated_norm_kernel,
        out_shape=jax.ShapeDtypeStruct((b, t, w), BF16),
        grid=(b, t // ROW_TILE),
        in_specs=[o_spec(0), o_spec(1),
                  pl.BlockSpec((None, ROW_TILE, w), lambda i, j: (i, j, 0)),
                  pl.BlockSpec((1, LANES), lambda i, j: (0, 0))],
        out_specs=pl.BlockSpec((None, ROW_TILE, w), lambda i, j: (i, j, 0)),
        compiler_params=_cparams("parallel", "parallel"),
        name="gated_norm",
    )(o_fwd, o_bwd, z, gain_row)


def _merge_kernel(h_ref, oa_ref, ob_ref, oc_ref, od_ref, wg_ref, wb_ref, wo_ref, x_ref, mod_ref, gains_ref,
                  wrh_ref, wrl_ref, br_ref, xn_ref, h2_ref, idx_ref, wgt_ref):
    d = x_ref.shape[1]
    subs = [pl.ds(n * MERGE_SUB, MERGE_SUB) for n in range(ROW_TILE // MERGE_SUB)]
    ys = []
    for rows in subs:
        h = h_ref[rows, :]
        merged = None
        for n, o_ref in enumerate((oa_ref, ob_ref, oc_ref, od_ref)):
            gate = jax.nn.sigmoid(_dot(h, wg_ref[:, n * d:(n + 1) * d]))
            term = gate * _dot(o_ref[rows, :], wb_ref[n])
            merged = term if merged is None else merged + term
        ys.append(_dot(merged.astype(BF16), wo_ref[...]))
    for rows, y in zip(subs, ys):
        x_new = x_ref[rows, :] + mod_ref[2:3, :] * (_rms(y) * gains_ref[1:2, :])
        xn_ref[rows, :] = x_new
        h2 = _rms(x_new) * gains_ref[2:3, :] * (1.0 + mod_ref[4:5, :]) + mod_ref[3:4, :]
        h2_ref[rows] = h2.reshape((MERGE_SUB,) + h2_ref.shape[1:])
        h2h, h2l = _split2(h2)
        logits = _dot(h2h, wrh_ref[...]) + _dot(h2l, wrh_ref[...]) + _dot(h2h, wrl_ref[...]) + br_ref[...]
        ne = logits.shape[1]
        lane = lax.broadcasted_iota(jnp.int32, logits.shape, 1)
        vals = []
        for kk in range(TOP_K):
            m = jnp.max(logits, axis=-1, keepdims=True)
            sel = jnp.min(jnp.where(logits == m, lane, ne), axis=-1, keepdims=True)
            sel = jnp.minimum(sel, ne - 1)
            idx_ref[rows, kk:kk + 1] = sel
            vals.append(m)
            logits = jnp.where(lane == sel, -jnp.inf, logits)
        es = [jnp.exp(vv - vals[0]) for vv in vals]
        tot = es[0] + es[1] + es[2] + es[3]
        for kk in range(TOP_K):
            wgt_ref[rows, kk:kk + 1] = es[kk] / tot


def _merge(h, branch_outs, wg, wb, wo, x, modtab, gains, w_router, b_router, nct, tile0):
    b, t, d = x.shape
    nt = t // ROW_TILE - tile0
    ne = w_router.shape[1]
    rows = lambda wd: pl.BlockSpec((None, ROW_TILE, wd), lambda i, j: (i, j + tile0, 0))
    bw = branch_outs[0].shape[2]
    wr_hi = w_router.astype(BF16)
    wr_lo = (w_router - wr_hi.astype(F32)).astype(BF16)
    return pl.pallas_call(
        _merge_kernel,
        out_shape=[jax.ShapeDtypeStruct((b, t, d), F32),
                   jax.ShapeDtypeStruct((b, t, d // LANES, LANES), F32),
                   jax.ShapeDtypeStruct((b, t, TOP_K), jnp.int32),
                   jax.ShapeDtypeStruct((b, t, TOP_K), F32)],
        grid=(b, nt),
        in_specs=[rows(d), rows(bw), rows(bw), rows(bw), rows(bw),
                  pl.BlockSpec(wg.shape, lambda i, j: (0, 0), pipeline_mode=pl.Buffered(1)),
                  pl.BlockSpec(wb.shape, lambda i, j: (0, 0, 0), pipeline_mode=pl.Buffered(1)),
                  pl.BlockSpec(wo.shape, lambda i, j: (0, 0), pipeline_mode=pl.Buffered(1)),
                  rows(d),
                  pl.BlockSpec((None, None, N_MOD, d), lambda i, j: (i, (j + tile0 >= nct).astype(jnp.int32), 0, 0)),
                  pl.BlockSpec(gains.shape, lambda i, j: (0, 0)),
                  pl.BlockSpec(w_router.shape, lambda i, j: (0, 0)),
                  pl.BlockSpec(w_router.shape, lambda i, j: (0, 0)),
                  pl.BlockSpec((1, ne), lambda i, j: (0, 0))],
        out_specs=[rows(d),
                   pl.BlockSpec((None, ROW_TILE, d // LANES, LANES), lambda i, j: (i, j + tile0, 0, 0)),
                   rows(TOP_K), rows(TOP_K)],
        compiler_params=_cparams("parallel", "parallel"),
        name="merge",
    )(h, *branch_outs, wg, wb, wo, x, modtab, gains, wr_hi, wr_lo, b_router.reshape(1, ne))


def _rank_kernel(idx_ref, rank_ref, count_ref, carry_ref, *, ne):
    @pl.when((pl.program_id(0) == 0) & (pl.program_id(1) == 0))
    def _():
        carry_ref[...] = jnp.zeros_like(carry_ref)

    idx = idx_ref[...]
    tm = idx.shape[0]
    lane = lax.broadcasted_iota(jnp.int32, (tm, ne), 1)
    onehots = [(lane == idx[:, kk:kk + 1]) for kk in range(TOP_K)]
    member = onehots[0] | onehots[1] | onehots[2] | onehots[3]
    i = lax.broadcasted_iota(jnp.int32, (tm, tm), 0)
    j = lax.broadcasted_iota(jnp.int32, (tm, tm), 1)
    before = _dot((j < i).astype(BF16), member.astype(BF16)) + carry_ref[...]
    for kk in range(TOP_K):
        rk = jnp.sum(jnp.where(onehots[kk], before, 0.0), axis=-1, keepdims=True)
        rank_ref[:, kk:kk + 1] = rk.astype(jnp.int32)
    carry_ref[...] = carry_ref[...] + jnp.sum(member.astype(F32), axis=0, keepdims=True)
    count_ref[...] = carry_ref[...].astype(jnp.int32)


def _moe_rank(top_idx, ne, tile0):
    b, t, _ = top_idx.shape
    rows = pl.BlockSpec((None, ROW_TILE, TOP_K), lambda i, j: (i, j + tile0, 0))
    return pl.pallas_call(
        functools.partial(_rank_kernel, ne=ne),
        out_shape=[jax.ShapeDtypeStruct((b, t, TOP_K), jnp.int32), jax.ShapeDtypeStruct((1, ne), jnp.int32)],
        grid=(b, t // ROW_TILE - tile0),
        in_specs=[rows],
        out_specs=[rows, pl.BlockSpec((1, ne), lambda i, j: (0, 0))],
        scratch_shapes=[pltpu.VMEM((1, ne), F32)],
        compiler_params=_cparams("arbitrary", "arbitrary"),
        name="moe_rank",
    )(top_idx)


def _row_copy(src, dst, sem):
    return pltpu.make_async_copy(src, dst, sem)


def _dispatch_kernel(dest_ref, h_ref, buf_in, buf_hbm, sem):
    del buf_in

    def issue(r, carry):
        for kk in range(TOP_K):
            _row_copy(h_ref.at[r], buf_hbm.at[dest_ref[0, r * TOP_K + kk]], sem).start()
        return carry

    lax.fori_loop(0, ROW_TILE, issue, 0)
    _row_copy(buf_hbm.at[pl.ds(0, ROW_TILE * TOP_K)], buf_hbm.at[pl.ds(0, ROW_TILE * TOP_K)], sem).wait()


def _moe_dispatch(h2, dest, n_rows, tile0):
    b, t, s, l = h2.shape
    tpb = t // ROW_TILE
    buf0 = jnp.zeros((n_rows, s, l), h2.dtype)
    dest2 = dest.reshape(b * tpb, 1, ROW_TILE * TOP_K)
    return pl.pallas_call(
        _dispatch_kernel,
        out_shape=jax.ShapeDtypeStruct((n_rows, s, l), h2.dtype),
        grid=(b, tpb - tile0),
        in_specs=[pl.BlockSpec((None, 1, ROW_TILE * TOP_K), lambda i, j: (i * tpb + j + tile0, 0, 0),
                               memory_space=pltpu.SMEM),
                  pl.BlockSpec((ROW_TILE, s, l), lambda i, j: (i * tpb + j + tile0, 0, 0)),
                  pl.BlockSpec(memory_space=pl.ANY)],
        out_specs=pl.BlockSpec(memory_space=pl.ANY),
        scratch_shapes=[pltpu.SemaphoreType.DMA],
        input_output_aliases={2: 0},
        compiler_params=pltpu.CompilerParams(dimension_semantics=("arbitrary", "arbitrary"),
                                             vmem_limit_bytes=VMEM_LIMIT, has_side_effects=True),
        name="moe_dispatch",
    )(dest2, h2.reshape(b * t, s, l), buf0)


def _ffn_kernel(te_ref, nu_ref, x_ref, wgu_ref, bgu_ref, wd_ref, bd_ref, y_ref):
    i = pl.program_id(0)
    ns = x_ref.shape[1]

    @pl.when(i < nu_ref[0])
    def _():
        subs = [pl.ds(h * MOE_SUB, MOE_SUB) for h in range(MOE_TILE // MOE_SUB)]
        gus = []
        for rows in subs:
            x = x_ref[rows].reshape(MOE_SUB, ns * LANES).astype(BF16)
            gus.append(_dot(x, wgu_ref[...]) + bgu_ref[...])
        for rows, gu in zip(subs, gus):
            f = gu.shape[1] // 2
            gate = jnp.minimum(gu[:, :f], SWIGLU_LIMIT)
            up = jnp.clip(gu[:, f:], -SWIGLU_LIMIT, SWIGLU_LIMIT)
            act = (up + 1.0) * gate * jax.nn.sigmoid(SWIGLU_ALPHA * gate)
            y = _dot(act.astype(BF16), wd_ref[...]) + bd_ref[...]
            y_ref[rows] = y.reshape(MOE_SUB, ns, LANES)

    @pl.when(i >= nu_ref[0])
    def _():
        y_ref[...] = jnp.zeros_like(y_ref)


def _moe_ffn(buf, tile_expert, n_used, wgu, bgu, wd, bd):
    n_rows, s, l = buf.shape
    ne, d, f2 = wgu.shape
    grid_spec = pltpu.PrefetchScalarGridSpec(
        num_scalar_prefetch=2,
        grid=(n_rows // MOE_TILE,),
        in_specs=[pl.BlockSpec((MOE_TILE, s, l), lambda i, te, nu: (i, 0, 0)),
                  pl.BlockSpec((None, d, f2), lambda i, te, nu: (te[i], 0, 0)),
                  pl.BlockSpec((None, 1, f2), lambda i, te, nu: (te[i], 0, 0)),
                  pl.BlockSpec((None, f2 // 2, d), lambda i, te, nu: (te[i], 0, 0)),
                  pl.BlockSpec((None, 1, d), lambda i, te, nu: (te[i], 0, 0))],
        out_specs=pl.BlockSpec((MOE_TILE, s, l), lambda i, te, nu: (i, 0, 0)),
    )
    return pl.pallas_call(
        _ffn_kernel,
        out_shape=jax.ShapeDtypeStruct((n_rows, s, l), F32),
        grid_spec=grid_spec,
        compiler_params=_cparams("arbitrary"),
        name="moe_ffn",
    )(tile_expert, n_used, buf, wgu, bgu.reshape(ne, 1, f2), wd, bd.reshape(ne, 1, d))


def _combine_kernel(dest_ref, next_ref, wgt_ref, y_hbm, x_ref, mod_ref, gain_ref, o_ref, gat_ref, acc_ref, sem):
    step = pl.program_id(0) * pl.num_programs(1) + pl.program_id(1)
    n_steps = pl.num_programs(0) * pl.num_programs(1)
    slot = step % 2

    def gather(idx_ref, to_slot):
        def issue(r, carry):
            for kk in range(TOP_K):
                _row_copy(y_hbm.at[idx_ref[0, r * TOP_K + kk]], gat_ref.at[to_slot, r * TOP_K + kk],
                          sem.at[to_slot]).start()
            return carry

        lax.fori_loop(0, ROW_TILE, issue, 0)

    @pl.when(step == 0)
    def _():
        gather(dest_ref, slot)

    @pl.when(step + 1 < n_steps)
    def _():
        gather(next_ref, 1 - slot)

    _row_copy(y_hbm.at[pl.ds(0, ROW_TILE * TOP_K)], gat_ref.at[slot], sem.at[slot]).wait()

    def mix(r, carry):
        acc = None
        for kk in range(TOP_K):
            term = wgt_ref[0, r * TOP_K + kk] * gat_ref[slot, r * TOP_K + kk]
            acc = term if acc is None else acc + term
        acc_ref[r] = acc
        return carry

    lax.fori_loop(0, ROW_TILE, mix, 0)
    f = acc_ref[...].reshape(x_ref.shape)
    o_ref[...] = x_ref[...] + mod_ref[5:6, :] * (_rms(f) * gain_ref[...])


def _moe_combine(ybuf, dest, wgt, x, modtab, gain_row, nct, tile0):
    b, t, d = x.shape
    tpb = t // ROW_TILE
    nt = tpb - tile0
    s, l = ybuf.shape[1:]
    dest2 = dest.reshape(b * tpb, 1, ROW_TILE * TOP_K)
    wgt2 = wgt.reshape(b * tpb, 1, ROW_TILE * TOP_K)
    tile_of = lambda i, j: i * tpb + j + tile0

    def next_tile(i, j):
        last = j == nt - 1
        return jnp.minimum(tile_of(jnp.where(last, i + 1, i), jnp.where(last, 0, j + 1)), b * tpb - 1)

    smem_rows = lambda f: pl.BlockSpec((None, 1, ROW_TILE * TOP_K), lambda i, j: (f(i, j), 0, 0),
                                       memory_space=pltpu.SMEM)
    return pl.pallas_call(
        _combine_kernel,
        out_shape=jax.ShapeDtypeStruct((b, t, d), F32),
        grid=(b, nt),
        in_specs=[smem_rows(tile_of), smem_rows(next_tile), smem_rows(tile_of),
                  pl.BlockSpec(memory_space=pl.ANY),
                  pl.BlockSpec((None, ROW_TILE, d), lambda i, j: (i, j + tile0, 0)),
                  pl.BlockSpec((None, None, N_MOD, d), lambda i, j: (i, (j + tile0 >= nct).astype(jnp.int32), 0, 0)),
                  pl.BlockSpec((1, d), lambda i, j: (0, 0))],
        out_specs=pl.BlockSpec((None, ROW_TILE, d), lambda i, j: (i, j + tile0, 0)),
        scratch_shapes=[pltpu.VMEM((2, ROW_TILE * TOP_K, s, l), F32),
                        pltpu.VMEM((ROW_TILE, s, l), F32),
                        pltpu.SemaphoreType.DMA((2,))],
        compiler_params=_cparams("arbitrary", "arbitrary"),
        name="moe_combine",
    )(dest2, dest2, wgt2, ybuf, x, modtab, gain_row)


def _deinterleave(width):
    idx = np.arange(width).reshape(-1, HEAD_DIM // 2, 2)
    return np.concatenate([idx[:, :, 0], idx[:, :, 1]], axis=1).reshape(-1)


def _rope_tables(ang, ctx_len):
    cos = jnp.cos(ang)
    sin = jnp.sin(ang)
    cos64 = jnp.concatenate([cos, cos], axis=-1)
    sin64 = jnp.concatenate([-sin, sin], axis=-1)
    reps = LANES // HEAD_DIM
    cos_t = jnp.concatenate([jnp.ones((ctx_len, HEAD_DIM), F32), cos64], axis=0)
    sin_t = jnp.concatenate([jnp.zeros((ctx_len, HEAD_DIM), F32), sin64], axis=0)
    return jnp.tile(cos_t, (1, reps)), jnp.tile(sin_t, (1, reps))


def _axial_angles(n_tok):
    rows = n_tok // GRID_W
    row = jnp.repeat(jnp.arange(rows), GRID_W).astype(F32)
    col = jnp.tile(jnp.arange(GRID_W), rows).astype(F32)
    half = HEAD_DIM // 2
    inv = ROPE_THETA ** (-jnp.arange(0, half, 2, dtype=F32) / half)
    return jnp.concatenate([row[:, None] * inv, col[:, None] * inv], axis=-1)


def _line_angles(n_tok):
    pos = jnp.arange(n_tok, dtype=F32)
    inv = ROPE_THETA ** (-jnp.arange(0, RET_QK_DIM, 2, dtype=F32) / RET_QK_DIM)
    return pos[:, None] * inv


def _layer(x, modtab, p, layer_idx, ctx_len, last):
    b, t, d = x.shape
    nct = ctx_len // ROW_TILE
    tile0 = nct if last else 0
    bwid = d // 2
    sizes = (bwid, GQA_KV_HEADS * HEAD_DIM, GQA_KV_HEADS * HEAD_DIM,
             bwid, bwid, bwid,
             RET_HEADS * RET_QK_DIM, RET_HEADS * RET_QK_DIM, bwid, bwid,
             3 * bwid, bwid, 2 * DN_HEADS, 2 * DN_HEADS,
             N_BRANCHES * d)
    cuts = np.concatenate([[0], np.cumsum(sizes)])
    w_in = p['w_in']
    col = lambda a, e: w_in[:, cuts[a]:cuts[e]]
    gains = p['norm_gain']
    ones_row = jnp.ones((1, LANES), F32)

    h = _modulate(x, modtab, gains[0:1], nct)

    seq = t - ctx_len
    cos_ax, sin_ax = _rope_tables(_axial_angles(seq), ctx_len)
    cos_ln, sin_ln = _rope_tables(_line_angles(seq), ctx_len)

    perm_q = _deinterleave(sizes[0])
    perm_k = _deinterleave(sizes[1])
    wa = jnp.concatenate([col(0, 1)[:, perm_q], col(1, 2)[:, perm_k], col(2, 3)], axis=1).astype(BF16)
    perm64 = _deinterleave(HEAD_DIM)
    qk_gain = p['gqa_qk_gain'].astype(F32)
    qscale = HEAD_DIM ** -0.5
    gain_a = jnp.concatenate([jnp.tile(qk_gain[0][perm64] * qscale, sizes[0] // HEAD_DIM),
                              jnp.tile(qk_gain[1][perm64], sizes[1] // HEAD_DIM)]).reshape(1, -1)
    qa, ka, va = _inproj(h, wa, cos_ax, sin_ax, gain_a,
                         [(sizes[0], BF16), (sizes[1], BF16), (sizes[2], BF16)], sizes[0] + sizes[1], True)
    oa = _gqa_attention(qa, ka, va, nct, ctx_len, tile0)

    perm_b = _deinterleave(bwid)
    wb_in = jnp.concatenate([col(3, 4)[:, perm_b] * qscale, col(4, 5)[:, perm_b], col(5, 6)], axis=1).astype(BF16)
    qb, kb, vb = _inproj(h, wb_in, cos_ax, sin_ax, ones_row,
                         [(bwid, BF16)] * 3, 2 * bwid, False)
    lam_init = 0.8 - 0.6 * math.exp(-0.3 * layer_idx)
    ob = _diff_attention(qb, kb, vb, p['diff_lambda'].astype(F32), p['diff_norm'].astype(F32).reshape(1, -1),
                         nct, ctx_len, tile0, lam_init)

    perm_c = _deinterleave(sizes[6])
    wc = jnp.concatenate([col(6, 7)[:, perm_c], col(7, 8)[:, perm_c] * (RET_QK_DIM ** -0.5), col(8, 10)],
                         axis=1).astype(BF16)
    qc, kc, vc, gc = _inproj(h, wc, cos_ln, sin_ln, ones_row,
                             [(sizes[6], BF16), (sizes[7], BF16), (bwid, BF16), (bwid, BF16)],
                             sizes[6] + sizes[7], False)
    log_gamma = jax.nn.log_sigmoid(p['ret_decay_logit'].astype(F32))
    oc_f, oc_b = _retention(qc, kc, vc, log_gamma, ctx_len)
    oc = _gated_norm(oc_f, oc_b, gc, ones_row)

    wd_in = col(10, 12).astype(BF16)
    w_ba = col(12, 14).astype(BF16)
    w_ba = jnp.pad(w_ba, ((0, 0), (0, LANES - w_ba.shape[1])))
    wd_all = jnp.concatenate([wd_in, w_ba], axis=1)
    qkv_d, z_d, ba_d = _inproj(h, wd_all, cos_ln, sin_ln, ones_row,
                               [(3 * bwid, F32), (bwid, BF16), (LANES, F32)], 0, False)
    nb = 2 * DN_HEADS
    rate_row = jnp.exp(p['dn_a_log'].astype(F32)).reshape(1, nb)
    bias_row = p['dn_dt_bias'].astype(F32).reshape(1, nb)
    qd, kd, vd, g_d, beta_d = _dn_prep(qkv_d, ba_d[:, :, :2 * nb], p['dn_conv_w'].astype(F32), rate_row, bias_row, nct)
    od_f, od_b = _dn_scan(*_dn_local(qd, kd, vd, g_d, beta_d), ctx_len)
    od = _gated_norm(od_f, od_b, z_d, p['dn_norm'].astype(F32).reshape(1, -1))

    wg = col(14, 15).astype(BF16)
    x_new, h2, top_idx, top_w = _merge(h, (oa, ob, oc, od), wg, p['w_branch'].astype(BF16),
                                       p['w_out'].astype(BF16), x, modtab, gains,
                                       p['w_router'].astype(F32), p['b_router'].astype(F32), nct, tile0)

    ne = p['w_router'].shape[1]
    n_routed = b * (t - tile0 * ROW_TILE)
    rank, counts = _moe_rank(top_idx, ne, tile0)
    counts = counts.reshape(ne)
    padded = (counts + MOE_TILE - 1) // MOE_TILE * MOE_TILE
    pad_ends = jnp.cumsum(padded)
    pad_starts = pad_ends - padded
    dest = pad_starts[jnp.clip(top_idx, 0, ne - 1)] + rank
    n_tiles = (n_routed * TOP_K) // MOE_TILE + ne
    tile_start = jnp.arange(n_tiles, dtype=jnp.int32) * MOE_TILE
    tile_expert = jnp.minimum(jnp.sum(pad_ends[None, :] <= tile_start[:, None], axis=1), ne - 1).astype(jnp.int32)
    n_used = (pad_ends[-1] // MOE_TILE).astype(jnp.int32).reshape(1)
    buf = _moe_dispatch(h2, dest, n_tiles * MOE_TILE, tile0)
    ybuf = _moe_ffn(buf, tile_expert, n_used, p['w_gate_up'].astype(BF16), p['b_gate_up'].astype(F32),
                    p['w_down'].astype(BF16), p['b_down'].astype(F32))
    return _moe_combine(ybuf, dest.reshape(b * t, TOP_K), top_w.reshape(b * t, TOP_K), x_new, modtab, gains[3:4],
                        nct, tile0)


def kernel(x, c, ctx, c_ctx, w_mod, b_mod, norm_gain, w_in, gqa_qk_gain, diff_lambda, diff_norm, ret_decay_logit,
           dn_conv_w, dn_a_log, dn_dt_bias, dn_norm, w_branch, w_out, w_router, b_router, w_gate_up, b_gate_up,
           w_down, b_down):
    b, seq, d = x.shape
    ctx_len = ctx.shape[1]
    depth = w_mod.shape[0]
    assert ctx_len % ROW_TILE == 0 and seq % ROW_TILE == 0 and seq % GRID_W == 0 and d % LANES == 0
    xa = jnp.concatenate([ctx, x], axis=1).astype(F32)
    c_all = jnp.concatenate([c, c_ctx[None, :]], axis=0).astype(F32)
    for l in range(depth):
        p = {
            'norm_gain': norm_gain[l].astype(F32), 'w_in': w_in[l], 'gqa_qk_gain': gqa_qk_gain[l],
            'diff_lambda': diff_lambda[l], 'diff_norm': diff_norm[l], 'ret_decay_logit': ret_decay_logit[l],
            'dn_conv_w': dn_conv_w[l], 'dn_a_log': dn_a_log[l], 'dn_dt_bias': dn_dt_bias[l], 'dn_norm': dn_norm[l],
            'w_branch': w_branch[l], 'w_out': w_out[l], 'w_router': w_router[l], 'b_router': b_router[l],
            'w_gate_up': w_gate_up[l], 'b_gate_up': b_gate_up[l], 'w_down': w_down[l], 'b_down': b_down[l],
        }
        mod = _mod_table(c_all, w_mod[l].astype(F32), b_mod[l].astype(F32))
        mod = mod.reshape(b + 1, N_MOD, d)
        modtab = jnp.stack([jnp.broadcast_to(mod[b], (b, N_MOD, d)), mod[:b]], axis=1)
        xa = _layer(xa, modtab, p, l, ctx_len, l == depth - 1)
    return xa[:, ctx_len:, :]
```

```python
import functools
import math

import numpy as np
import jax
import jax.numpy as jnp
from jax import lax
from jax.experimental import pallas as pl
from jax.experimental.pallas import tpu as pltpu

F32 = jnp.float32
BF16 = jnp.bfloat16

GRID_W = 64
RMS_EPS = 1e-6
ROPE_THETA = 10000.0
HEAD_DIM = 64
GQA_KV_HEADS = 2
GQA_GROUP = 4
DIFF_HEADS = 4
RET_HEADS = 4
RET_QK_DIM = 64
RET_V_DIM = 128
DN_HEADS = 4
DN_HEAD_DIM = 128
DN_CONV = 5
DN_CHUNK = 64
TOP_K = 4
SWIGLU_LIMIT = 7.0
SWIGLU_ALPHA = 1.702
N_MOD = 6
N_BRANCHES = 4

LANES = 128
SUBLANES = 8
ROW_TILE = 256
RET_CHUNK = 128
MERGE_SUB = 128
ATT_KEY_BLOCK = 256
ATT_SKEW = 3
DN_BATCH_CHUNKS = 4
MOE_TILE = 512
MOE_SUB = 256
VMEM_LIMIT = 56 * 1024 * 1024


def _cparams(*sem):
    return pltpu.CompilerParams(dimension_semantics=sem, vmem_limit_bytes=VMEM_LIMIT)


def _const_spec(shape):
    nd = len(shape)
    return pl.BlockSpec(shape, lambda *_: (0,) * nd)


def _rms(x):
    return x * lax.rsqrt(jnp.mean(x * x, axis=-1, keepdims=True) + RMS_EPS)


def _silu(x):
    return x * jax.nn.sigmoid(x)


def _dot(a, b):
    return jnp.dot(a, b, preferred_element_type=F32)


def _dot_nt(a, b):
    return lax.dot_general(a, b, (((1,), (1,)), ((), ())), preferred_element_type=F32)


def _dot_tn(a, b):
    return lax.dot_general(a, b, (((0,), (0,)), ((), ())), preferred_element_type=F32)


def _mod_kernel(c_ref, w_ref, b_ref, o_ref):
    o_ref[...] = _dot(_silu(c_ref[...]), w_ref[...]) + b_ref[...]


def _mod_table(c_all, w_mod, b_mod):
    m, d = c_all.shape
    n = w_mod.shape[1]
    tn = d
    return pl.pallas_call(
        _mod_kernel,
        out_shape=jax.ShapeDtypeStruct((m, n), F32),
        grid=(n // tn,),
        in_specs=[pl.BlockSpec((m, d), lambda j: (0, 0)),
                  pl.BlockSpec((d, tn), lambda j: (0, j)),
                  pl.BlockSpec((1, tn), lambda j: (0, j))],
        out_specs=pl.BlockSpec((m, tn), lambda j: (0, j)),
        compiler_params=_cparams("parallel"),
        name="mod_table",
    )(c_all, w_mod, b_mod.reshape(1, n))


def _modulate_kernel(x_ref, mod_ref, gain_ref, h_ref):
    xn = _rms(x_ref[...]) * gain_ref[...]
    h = xn * (1.0 + mod_ref[1:2, :]) + mod_ref[0:1, :]
    h_ref[...] = h.astype(h_ref.dtype)


def _modulate(x, modtab, gain_row, nct):
    b, t, d = x.shape
    return pl.pallas_call(
        _modulate_kernel,
        out_shape=jax.ShapeDtypeStruct((b, t, d), BF16),
        grid=(b, t // ROW_TILE),
        in_specs=[pl.BlockSpec((None, ROW_TILE, d), lambda i, j: (i, j, 0)),
                  pl.BlockSpec((None, None, N_MOD, d), lambda i, j: (i, (j >= nct).astype(jnp.int32), 0, 0)),
                  pl.BlockSpec((1, d), lambda i, j: (0, 0))],
        out_specs=pl.BlockSpec((None, ROW_TILE, d), lambda i, j: (i, j, 0)),
        compiler_params=_cparams("parallel", "parallel"),
        name="modulate",
    )(x, modtab, gain_row)


def _swap_halves(y):
    lane = lax.broadcasted_iota(jnp.int32, y.shape, 1)
    first = (lane % HEAD_DIM) < (HEAD_DIM // 2)
    return jnp.where(first, pltpu.roll(y, LANES - HEAD_DIM // 2, 1), pltpu.roll(y, HEAD_DIM // 2, 1))


def _inproj_kernel(h_ref, w_ref, cos_ref, sin_ref, gain_ref, seg_ref, *out_refs, widths, n_rope, do_norm):
    h = h_ref[...]
    cos = cos_ref[...]
    sin = sin_ref[...]
    col = 0
    for o_ref, wd in zip(out_refs, widths):
        u = _dot(h, w_ref[:, col:col + wd])
        for j in range(0, wd, LANES):
            wj = min(LANES, wd - j)
            y = u[:, j: j + wj]
            if col + j < n_rope:
                if do_norm:
                    ms = _dot((y * y).astype(BF16), seg_ref[...])
                    y = y * lax.rsqrt(ms + RMS_EPS) * gain_ref[:, col + j: col + j + wj]
                y = y * cos + _swap_halves(y) * sin
            o_ref[:, j: j + wj] = y.astype(o_ref.dtype)
        col += wd


def _inproj(h, w, cos, sin, gain_row, outs, n_rope, do_norm):
    b, t, d = h.shape
    c = w.shape[1]
    widths = tuple(o[0] for o in outs)
    assert sum(widths) == c
    seg = np.kron(np.eye(LANES // HEAD_DIM), np.full((HEAD_DIM, HEAD_DIM), 1.0 / HEAD_DIM))
    seg = jnp.asarray(seg, BF16)
    kern = functools.partial(_inproj_kernel, widths=widths, n_rope=n_rope, do_norm=do_norm)
    return pl.pallas_call(
        kern,
        out_shape=[jax.ShapeDtypeStruct((b, t, wd), dt) for wd, dt in outs],
        grid=(b, t // ROW_TILE),
        in_specs=[pl.BlockSpec((None, ROW_TILE, d), lambda i, j: (i, j, 0)),
                  pl.BlockSpec((d, c), lambda i, j: (0, 0)),
                  pl.BlockSpec((ROW_TILE, LANES), lambda i, j: (j, 0)),
                  pl.BlockSpec((ROW_TILE, LANES), lambda i, j: (j, 0)),
                  pl.BlockSpec(gain_row.shape, lambda i, j: (0, 0)),
                  pl.BlockSpec((LANES, LANES), lambda i, j: (0, 0))],
        out_specs=[pl.BlockSpec((None, ROW_TILE, wd), lambda i, j: (i, j, 0)) for wd, _ in outs],
        compiler_params=_cparams("parallel", "parallel"),
        name="inproj",
    )(h, w, cos, sin, gain_row, seg)


def _softmax_attend(n_maps, nk, query, keys, values, emit, s_ref):
    kb = min(ATT_KEY_BLOCK, nk)
    blocks = [slice(j * kb, (j + 1) * kb) for j in range(nk // kb)]
    rows = s_ref.shape[1]
    lane_pieces = lambda a: [a[:, c:c + LANES] for c in range(0, kb, LANES)]

    def pass1_block(n, q, blk, m_run):
        s = _dot_nt(q, keys(n, blk))
        s_ref[n % 2, :, blk] = s
        for piece in lane_pieces(s):
            m_run = jnp.maximum(m_run, piece)
        return m_run

    neg = jnp.full((rows, LANES), -jnp.inf, F32)
    q_next = query(0)
    m_run = neg
    for blk in blocks:
        m_run = pass1_block(0, q_next, blk, m_run)
    for n in range(n_maps):
        m = jnp.max(m_run, axis=-1, keepdims=True)
        m_run = neg
        if n + 1 < n_maps:
            q_next = query(n + 1)
        l_run = jnp.zeros((rows, LANES), F32)
        acc = None
        ahead = blocks[:ATT_SKEW] if n + 1 < n_maps else []
        for blk in ahead:
            m_run = pass1_block(n + 1, q_next, blk, m_run)
        for idx, blk in enumerate(blocks):
            if n + 1 < n_maps and idx + ATT_SKEW < len(blocks):
                m_run = pass1_block(n + 1, q_next, blocks[idx + ATT_SKEW], m_run)
            p = jnp.exp(s_ref[n % 2, :, blk] - m)
            for piece in lane_pieces(p):
                l_run = l_run + piece
            pv = _dot(p.astype(BF16), values(n, blk))
            acc = pv if acc is None else acc + pv
        emit(n, acc / jnp.sum(l_run, axis=-1, keepdims=True))


def _gqa_kernel(q_ref, k_ref, v_ref, o_ref, s_ref, *, nct, ctx_len, tile0):
    t = pl.program_id(1) + tile0

    def run(nk):
        head = lambda n: slice(n * HEAD_DIM, (n + 1) * HEAD_DIM)
        kv_head = lambda n: head(n // GQA_GROUP)

        def emit(n, o):
            o_ref[:, head(n)] = o.astype(o_ref.dtype)

        _softmax_attend(GQA_KV_HEADS * GQA_GROUP, nk,
                        lambda n: q_ref[:, head(n)],
                        lambda n, blk: k_ref[blk, kv_head(n)],
                        lambda n, blk: v_ref[blk, kv_head(n)],
                        emit, s_ref)

    if tile0 < nct:
        @pl.when(t < nct)
        def _():
            run(ctx_len)

    @pl.when(t >= nct)
    def _():
        run(k_ref.shape[0])


def _gqa_attention(q, k, v, nct, ctx_len, tile0):
    b, t, cq = q.shape
    nt = t // ROW_TILE - tile0
    kern = functools.partial(_gqa_kernel, nct=nct, ctx_len=ctx_len, tile0=tile0)
    return pl.pallas_call(
        kern,
        out_shape=jax.ShapeDtypeStruct((b, t, cq), BF16),
        grid=(b, nt),
        in_specs=[pl.BlockSpec((None, ROW_TILE, cq), lambda i, j: (i, j + tile0, 0)),
                  pl.BlockSpec((None, t, k.shape[2]), lambda i, j: (i, 0, 0)),
                  pl.BlockSpec((None, t, v.shape[2]), lambda i, j: (i, 0, 0))],
        out_specs=pl.BlockSpec((None, ROW_TILE, cq), lambda i, j: (i, j + tile0, 0)),
        scratch_shapes=[pltpu.VMEM((2, ROW_TILE, t), F32)],
        compiler_params=_cparams("parallel", "parallel"),
        name="gqa_attention",
    )(q, k, v)


def _diff_kernel(q_ref, k_ref, v_ref, lam_ref, gain_ref, o_ref, s_ref, *, nct, ctx_len, tile0, lam_init):
    t = pl.program_id(1) + tile0
    lp = lam_ref[...]
    lam = (jnp.exp(jnp.sum(lp[0:1, :] * lp[1:2, :], axis=-1, keepdims=True))
           - jnp.exp(jnp.sum(lp[2:3, :] * lp[3:4, :], axis=-1, keepdims=True)) + lam_init)
    dv = 2 * HEAD_DIM

    def run(nk):
        qk_cols = lambda n: slice(n * HEAD_DIM, (n + 1) * HEAD_DIM)
        v_cols = lambda n: slice((n // 2) * dv, (n // 2 + 1) * dv)
        first = {}

        def emit(n, o):
            if n % 2 == 0:
                first[n // 2] = o
            else:
                out = _rms(first.pop(n // 2) - lam * o) * gain_ref[...] * (1.0 - lam_init)
                o_ref[:, v_cols(n)] = out.astype(o_ref.dtype)

        _softmax_attend(2 * DIFF_HEADS, nk,
                        lambda n: q_ref[:, qk_cols(n)],
                        lambda n, blk: k_ref[blk, qk_cols(n)],
                        lambda n, blk: v_ref[blk, v_cols(n)],
                        emit, s_ref)

    if tile0 < nct:
        @pl.when(t < nct)
        def _():
            run(ctx_len)

    @pl.when(t >= nct)
    def _():
        run(k_ref.shape[0])


def _diff_attention(q, k, v, lam_params, gain_row, nct, ctx_len, tile0, lam_init):
    b, t, cq = q.shape
    nt = t // ROW_TILE - tile0
    kern = functools.partial(_diff_kernel, nct=nct, ctx_len=ctx_len, tile0=tile0, lam_init=lam_init)
    return pl.pallas_call(
        kern,
        out_shape=jax.ShapeDtypeStruct((b, t, cq), BF16),
        grid=(b, nt),
        in_specs=[pl.BlockSpec((None, ROW_TILE, cq), lambda i, j: (i, j + tile0, 0)),
                  pl.BlockSpec((None, t, cq), lambda i, j: (i, 0, 0)),
                  pl.BlockSpec((None, t, cq), lambda i, j: (i, 0, 0)),
                  pl.BlockSpec(lam_params.shape, lambda i, j: (0, 0)),
                  pl.BlockSpec(gain_row.shape, lambda i, j: (0, 0))],
        out_specs=pl.BlockSpec((None, ROW_TILE, cq), lambda i, j: (i, j + tile0, 0)),
        scratch_shapes=[pltpu.VMEM((2, ROW_TILE, t), F32)],
        compiler_params=_cparams("parallel", "parallel"),
        name="diff_attention",
    )(q, k, v, lam_params, gain_row)


def _scan_chunk(d, s, n_ctx_chunks, n_chunks):
    rev = jnp.where(s < n_ctx_chunks, n_ctx_chunks - 1 - s, n_chunks + n_ctx_chunks - 1 - s)
    return jnp.where(d == 0, s, rev)


def _ret_kernel(lg_ref, qf_ref, kf_ref, vf_ref, qb_ref, kb_ref, vb_ref, of_ref, ob_ref, state_ref):
    c = RET_CHUNK
    nh = RET_HEADS

    @pl.when(pl.program_id(1) == 0)
    def _():
        state_ref[...] = jnp.zeros_like(state_ref)

    i = lax.broadcasted_iota(jnp.int32, (c, c), 0)
    j = lax.broadcasted_iota(jnp.int32, (c, c), 1)
    r = lax.broadcasted_iota(jnp.int32, (c, 1), 0)
    lg = jnp.stack([jnp.full((1, 1), lg_ref[d, hd], F32) for d in range(2) for hd in range(nh)])
    rel = jnp.stack([i - j] * nh + [j - i] * nh)
    intra = jnp.where(rel >= 0, jnp.exp(lg * jnp.maximum(rel, 0).astype(F32)), 0.0)
    pos_q = jnp.stack([r + 1] * nh + [c - r] * nh).astype(F32)
    pos_k = jnp.stack([c - 1 - r] * nh + [r] * nh).astype(F32)

    def stacked(refs, width):
        return jnp.stack([ref[:, hd * width:(hd + 1) * width] for ref in refs for hd in range(nh)])

    q = stacked((qf_ref, qb_ref), RET_QK_DIM)
    k = stacked((kf_ref, kb_ref), RET_QK_DIM)
    v = stacked((vf_ref, vb_ref), RET_V_DIM)
    st = state_ref[...]
    scores = _bdot_nt(q, k) * intra
    qd = (q.astype(F32) * jnp.exp(lg * pos_q)).astype(BF16)
    o = _bdot(scores.astype(BF16), v) + _bdot(qd, st.astype(BF16))
    kd = (k.astype(F32) * jnp.exp(lg * pos_k)).astype(BF16)
    state_ref[...] = st * jnp.exp(lg * c) + _bdot_tn(kd, v)
    for n in range(2 * nh):
        d, hd = divmod(n, nh)
        (of_ref, ob_ref)[d][:, hd * RET_V_DIM:(hd + 1) * RET_V_DIM] = o[n]


def _retention(q, k, v, log_gamma, ctx_len):
    b, t, _ = q.shape
    n = t // RET_CHUNK
    nc = ctx_len // RET_CHUNK

    def specs(d):
        rows = lambda a: pl.BlockSpec((None, RET_CHUNK, a.shape[2]), lambda i, s: (i, _scan_chunk(d, s, nc, n), 0))
        return [rows(q), rows(k), rows(v)]

    return pl.pallas_call(
        _ret_kernel,
        out_shape=[jax.ShapeDtypeStruct((b, t, v.shape[2]), F32)] * 2,
        grid=(b, n),
        in_specs=[pl.BlockSpec(memory_space=pltpu.SMEM)] + specs(0) + specs(1),
        out_specs=[specs(0)[2], specs(1)[2]],
        scratch_shapes=[pltpu.VMEM((2 * RET_HEADS, RET_QK_DIM, RET_V_DIM), F32)],
        compiler_params=_cparams("parallel", "arbitrary"),
        name="retention_scan",
    )(log_gamma, q, k, v, q, k, v)


def _dn_prep_kernel(prev_ref, cur_ref, next_ref, cw_ref, ba_ref, rate_ref, bias_ref,
                    q_ref, k_ref, v_ref, g_ref, beta_ref, *, nct, n_tiles):
    t = pl.program_id(1)
    pad = DN_CONV // 2
    first = (t == 0) | (t == nct)
    last = (t == nct - 1) | (t == n_tiles - 1)
    prev = jnp.where(first, 0.0, prev_ref[...])
    nxt = jnp.where(last, 0.0, next_ref[...])
    ext = jnp.concatenate([prev, cur_ref[...], nxt], axis=0)
    rows = ext.shape[0]
    acc = None
    for tap in range(DN_CONV):
        sh = (pad - tap) % rows
        x = ext if sh == 0 else pltpu.roll(ext, sh, 0)
        term = x[SUBLANES:SUBLANES + ROW_TILE, :] * cw_ref[tap:tap + 1, :]
        acc = term if acc is None else acc + term
    y = _silu(acc)
    hw = DN_HEADS * DN_HEAD_DIM
    for hd in range(DN_HEADS):
        sl = slice(hd * DN_HEAD_DIM, (hd + 1) * DN_HEAD_DIM)
        qh = y[:, sl]
        kh = y[:, hw + hd * DN_HEAD_DIM: hw + (hd + 1) * DN_HEAD_DIM]
        q_ref[:, sl] = qh * lax.rsqrt(jnp.sum(qh * qh, axis=-1, keepdims=True) + RMS_EPS) * (DN_HEAD_DIM ** -0.5)
        k_ref[:, sl] = kh * lax.rsqrt(jnp.sum(kh * kh, axis=-1, keepdims=True) + RMS_EPS)
    v_ref[...] = y[:, 2 * hw:3 * hw]
    nb = 2 * DN_HEADS
    ba = ba_ref[...]
    beta_ref[...] = jax.nn.sigmoid(ba[:, 0:nb])
    a = ba[:, nb:2 * nb] + bias_ref[...]
    softplus = jnp.maximum(a, 0.0) + jnp.log(1.0 + jnp.exp(-jnp.abs(a)))
    g_ref[...] = -rate_ref[...] * softplus


def _dn_prep(qkv, ba, conv_w, rate_row, bias_row, nct):
    b, t, c = qkv.shape
    n_tiles = t // ROW_TILE
    hb = ROW_TILE // SUBLANES
    last_blk = t // SUBLANES - 1
    hw = DN_HEADS * DN_HEAD_DIM
    nb = 2 * DN_HEADS
    kern = functools.partial(_dn_prep_kernel, nct=nct, n_tiles=n_tiles)
    row_spec = lambda wd: pl.BlockSpec((None, ROW_TILE, wd), lambda i, j: (i, j, 0))
    return pl.pallas_call(
        kern,
        out_shape=[jax.ShapeDtypeStruct((b, t, hw), F32)] * 3 + [jax.ShapeDtypeStruct((b, t, nb), F32)] * 2,
        grid=(b, n_tiles),
        in_specs=[pl.BlockSpec((None, SUBLANES, c), lambda i, j: (i, jnp.maximum(j * hb - 1, 0), 0)),
                  row_spec(c),
                  pl.BlockSpec((None, SUBLANES, c), lambda i, j: (i, jnp.minimum((j + 1) * hb, last_blk), 0)),
                  pl.BlockSpec(conv_w.shape, lambda i, j: (0, 0)),
                  row_spec(2 * nb),
                  pl.BlockSpec((1, nb), lambda i, j: (0, 0)),
                  pl.BlockSpec((1, nb), lambda i, j: (0, 0))],
        out_specs=[row_spec(hw)] * 3 + [row_spec(nb)] * 2,
        compiler_params=_cparams("parallel", "parallel"),
        name="deltanet_prep",
    )(qkv, qkv, qkv, conv_w, ba, rate_row, bias_row)


def _split2(x):
    hi = x.astype(BF16)
    lo = (x - hi.astype(F32)).astype(BF16)
    return hi, lo


def _bdot(a, b):
    return lax.dot_general(a, b, (((2,), (1,)), ((0,), (0,))), preferred_element_type=F32)


def _bdot_nt(a, b):
    return lax.dot_general(a, b, (((2,), (2,)), ((0,), (0,))), preferred_element_type=F32)


def _bdot_tn(a, b):
    return lax.dot_general(a, b, (((1,), (1,)), ((0,), (0,))), preferred_element_type=F32)


def _bdot_bf(a, b):
    return _bdot(a.astype(BF16), b.astype(BF16))


def _bdot_01(a01_bf16, x):
    xh, xl = _split2(x)
    return _bdot(a01_bf16, xh) + _bdot(a01_bf16, xl)


def _dn_local_kernel(q_ref, k_ref, v_ref, g_ref, beta_ref, u_ref, w_ref, qg_ref, kd_ref, attn_ref, el_ref):
    c = DN_CHUNK
    dh = DN_HEAD_DIM
    i = lax.broadcasted_iota(jnp.int32, (c, c), 0)
    j = lax.broadcasted_iota(jnp.int32, (c, c), 1)
    eye = (i == j).astype(F32)

    nh = DN_HEADS
    nu = 2 * nh * DN_BATCH_CHUNKS
    rel = jnp.stack(([i - j] * nh + [j - i] * nh) * DN_BATCH_CHUNKS)
    incl = rel >= 0
    strict = rel > 0
    tri = incl.astype(BF16)

    def chunk(ci, carry):
        r0 = pl.multiple_of(ci * (c * DN_BATCH_CHUNKS), c * DN_BATCH_CHUNKS)
        rows_of = [pl.ds(r0 + m * c, c) for m in range(DN_BATCH_CHUNKS)]
        g_all = [g_ref[rows, :] for rows in rows_of]
        b_all = [beta_ref[rows, :] for rows in rows_of]
        g = jnp.stack([ga[:, n:n + 1] for ga in g_all for n in range(2 * nh)])
        beta = jnp.stack([ba[:, n:n + 1] for ba in b_all for n in range(2 * nh)])
        heads = lambda ref: jnp.stack([ref[rows, hd * dh:(hd + 1) * dh]
                                       for rows in rows_of for _ in range(2) for hd in range(nh)])
        q = heads(q_ref)
        k = heads(k_ref)
        v = heads(v_ref)
        gc = _bdot_01(tri, jnp.broadcast_to(g, (nu, c, dh)))
        dmat = _bdot_01(tri, jnp.where(strict, jnp.broadcast_to(g, (nu, c, c)), 0.0))
        decay = jnp.where(incl, jnp.exp(dmat), 0.0)
        kb = k * beta
        kbf = k.astype(BF16)
        a = jnp.where(strict, -(_bdot_nt(kb.astype(BF16), kbf) * decay), 0.0)
        tinv = eye + a
        p = a
        for _ in range(5):
            pb = p.astype(BF16)
            p = _bdot(pb, pb)
            tinv = tinv + _bdot_bf(tinv, p)
        egc = jnp.exp(gc)
        tb = tinv.astype(BF16)
        u = _bdot(tb, (v * beta).astype(BF16))
        w = _bdot(tb, (kb * egc).astype(BF16))
        attn = jnp.where(incl, _bdot_nt(q.astype(BF16), kbf) * decay, 0.0)
        ends = [c - 1 if (n // nh) % 2 == 0 else 0 for n in range(nu)]
        last = jnp.concatenate([gc[n:n + 1, e:e + 1, :] for n, e in enumerate(ends)], axis=0)
        kd = k * jnp.exp(last - gc)
        qg = q * egc
        el = jnp.exp(last)
        for n in range(nu):
            m, rest = divmod(n, 2 * nh)
            d, hd = divmod(rest, nh)
            rows = rows_of[m]
            sl = slice(hd * dh, (hd + 1) * dh)
            u_ref[d, rows, sl] = u[n]
            w_ref[d, rows, sl] = w[n].astype(w_ref.dtype)
            attn_ref[d, rows, hd * c:(hd + 1) * c] = attn[n].astype(attn_ref.dtype)
            kd_ref[d, rows, sl] = kd[n].astype(kd_ref.dtype)
            qg_ref[d, rows, sl] = qg[n].astype(qg_ref.dtype)
            el_ref[d, ci * DN_BATCH_CHUNKS + m, :, sl] = el[n]
        return carry

    lax.fori_loop(0, ROW_TILE // (c * DN_BATCH_CHUNKS), chunk, 0)


def _dn_local(q, k, v, g, beta):
    b, t, hw = q.shape
    cpt = ROW_TILE // DN_CHUNK
    rows = lambda wd: pl.BlockSpec((None, ROW_TILE, wd), lambda i, j: (i, j, 0))
    rows2 = lambda wd: pl.BlockSpec((2, None, ROW_TILE, wd), lambda i, j: (0, i, j, 0))
    aw = DN_HEADS * DN_CHUNK
    return pl.pallas_call(
        _dn_local_kernel,
        out_shape=[jax.ShapeDtypeStruct((2, b, t, hw), F32),
                   jax.ShapeDtypeStruct((2, b, t, hw), BF16),
                   jax.ShapeDtypeStruct((2, b, t, hw), BF16),
                   jax.ShapeDtypeStruct((2, b, t, hw), BF16),
                   jax.ShapeDtypeStruct((2, b, t, aw), BF16),
                   jax.ShapeDtypeStruct((2, b, t // DN_CHUNK, 1, hw), F32)],
        grid=(b, t // ROW_TILE),
        in_specs=[rows(hw), rows(hw), rows(hw), rows(g.shape[2]), rows(g.shape[2])],
        out_specs=[rows2(hw), rows2(hw), rows2(hw), rows2(hw), rows2(aw),
                   pl.BlockSpec((2, None, cpt, 1, hw), lambda i, j: (0, i, j, 0, 0))],
        compiler_params=_cparams("parallel", "parallel"),
        name="deltanet_local",
    )(q, k, v, g, beta)


def _dn_scan_kernel(*refs):
    ins = refs[:12]
    of_ref, ob_ref, state_ref = refs[12:]
    dh = DN_HEAD_DIM
    c = DN_CHUNK

    @pl.when(pl.program_id(1) == 0)
    def _():
        state_ref[...] = jnp.zeros_like(state_ref)

    nh = DN_HEADS

    def stacked(k, width):
        return jnp.stack([ins[6 * d + k][:, hd * width:(hd + 1) * width] for d in range(2) for hd in range(nh)])

    st = state_ref[...]
    stb = st.astype(BF16)
    v_new = stacked(0, dh) - _bdot(stacked(1, dh), stb)
    vnb = v_new.astype(BF16)
    o = _bdot(stacked(2, dh), stb) + _bdot(stacked(4, c), vnb)
    state_ref[...] = st * stacked(5, dh) + _bdot_tn(stacked(3, dh), vnb)
    for n in range(2 * nh):
        d, hd = divmod(n, nh)
        (of_ref, ob_ref)[d][:, hd * dh:(hd + 1) * dh] = o[n]


def _dn_scan(u, w, qg, kd, attn, el, ctx_len):
    _, b, t, hw = u.shape
    n = t // DN_CHUNK
    nc = ctx_len // DN_CHUNK
    aw = attn.shape[3]

    def specs(d):
        rows = lambda wd: pl.BlockSpec((None, None, DN_CHUNK, wd), lambda i, s: (d, i, _scan_chunk(d, s, nc, n), 0))
        return [rows(hw), rows(hw), rows(hw), rows(hw), rows(aw),
                pl.BlockSpec((None, None, None, 1, hw), lambda i, s: (d, i, _scan_chunk(d, s, nc, n), 0, 0))]

    out = lambda d: pl.BlockSpec((None, DN_CHUNK, hw), lambda i, s: (i, _scan_chunk(d, s, nc, n), 0))
    args = (u, w, qg, kd, attn, el)
    return pl.pallas_call(
        _dn_scan_kernel,
        out_shape=[jax.ShapeDtypeStruct((b, t, hw), F32)] * 2,
        grid=(b, n),
        in_specs=specs(0) + specs(1),
        out_specs=[out(0), out(1)],
        scratch_shapes=[pltpu.VMEM((2 * DN_HEADS, DN_HEAD_DIM, DN_HEAD_DIM), F32)],
        compiler_params=_cparams("parallel", "arbitrary"),
        name="deltanet_scan",
    )(*args, *args)


def _gated_heads(o_fwd, o_bwd, z, gain):
    o = o_fwd + o_bwd
    pieces = []
    for j in range(0, o.shape[1], LANES):
        oh = _rms(o[:, j:j + LANES]) * gain
        pieces.append((oh * _silu(z[:, j:j + LANES].astype(F32))).astype(BF16))
    return jnp.concatenate(pieces, axis=-1)


def _merge_kernel(h_ref, oa_ref, ob_ref, ocf_ref, ocb_ref, gc_ref, odf_ref, odb_ref, zd_ref, dng_ref,
                  wg_ref, wb_ref, wo_ref, x_ref, mod_ref, gains_ref,
                  wrh_ref, wrl_ref, br_ref, xn_ref, h2_ref, idx_ref, wgt_ref):
    d = x_ref.shape[1]
    subs = [pl.ds(n * MERGE_SUB, MERGE_SUB) for n in range(ROW_TILE // MERGE_SUB)]
    ones = jnp.ones((1, LANES), F32)
    ys = []
    for rows in subs:
        h = h_ref[rows, :]
        branch = (lambda: oa_ref[rows, :],
                  lambda: ob_ref[rows, :],
                  lambda: _gated_heads(ocf_ref[rows, :], ocb_ref[rows, :], gc_ref[rows, :], ones),
                  lambda: _gated_heads(odf_ref[rows, :], odb_ref[rows, :], zd_ref[rows, :], dng_ref[...]))
        merged = None
        for n, o_of in enumerate(branch):
            gate = jax.nn.sigmoid(_dot(h, wg_ref[:, n * d:(n + 1) * d]))
            term = gate * _dot(o_of(), wb_ref[n])
            merged = term if merged is None else merged + term
        ys.append(_dot(merged.astype(BF16), wo_ref[...]))
    for rows, y in zip(subs, ys):
        x_new = x_ref[rows, :] + mod_ref[2:3, :] * (_rms(y) * gains_ref[1:2, :])
        xn_ref[rows, :] = x_new
        h2 = _rms(x_new) * gains_ref[2:3, :] * (1.0 + mod_ref[4:5, :]) + mod_ref[3:4, :]
        h2_ref[rows] = h2.reshape((MERGE_SUB,) + h2_ref.shape[1:])
        h2h, h2l = _split2(h2)
        logits = _dot(h2h, wrh_ref[...]) + _dot(h2l, wrh_ref[...]) + _dot(h2h, wrl_ref[...]) + br_ref[...]
        ne = logits.shape[1]
        lane = lax.broadcasted_iota(jnp.int32, logits.shape, 1)
        vals = []
        for kk in range(TOP_K):
            m = jnp.max(logits, axis=-1, keepdims=True)
            sel = jnp.min(jnp.where(logits == m, lane, ne), axis=-1, keepdims=True)
            sel = jnp.minimum(sel, ne - 1)
            idx_ref[rows, kk:kk + 1] = sel
            vals.append(m)
            logits = jnp.where(lane == sel, -jnp.inf, logits)
        es = [jnp.exp(vv - vals[0]) for vv in vals]
        tot = es[0] + es[1] + es[2] + es[3]
        for kk in range(TOP_K):
            wgt_ref[rows, kk:kk + 1] = es[kk] / tot


def _merge(h, branch_ins, dn_gain_row, wg, wb, wo, x, modtab, gains, w_router, b_router, nct, tile0):
    b, t, d = x.shape
    nt = t // ROW_TILE - tile0
    ne = w_router.shape[1]
    rows = lambda wd: pl.BlockSpec((None, ROW_TILE, wd), lambda i, j: (i, j + tile0, 0))
    bw = branch_ins[0].shape[2]
    wr_hi = w_router.astype(BF16)
    wr_lo = (w_router - wr_hi.astype(F32)).astype(BF16)
    return pl.pallas_call(
        _merge_kernel,
        out_shape=[jax.ShapeDtypeStruct((b, t, d), F32),
                   jax.ShapeDtypeStruct((b, t, d // LANES, LANES), F32),
                   jax.ShapeDtypeStruct((b, t, TOP_K), jnp.int32),
                   jax.ShapeDtypeStruct((b, t, TOP_K), F32)],
        grid=(b, nt),
        in_specs=[rows(d)] + [rows(bw)] * len(branch_ins)
                 + [pl.BlockSpec((1, LANES), lambda i, j: (0, 0)),
                  pl.BlockSpec(wg.shape, lambda i, j: (0, 0), pipeline_mode=pl.Buffered(1)),
                  pl.BlockSpec(wb.shape, lambda i, j: (0, 0, 0), pipeline_mode=pl.Buffered(1)),
                  pl.BlockSpec(wo.shape, lambda i, j: (0, 0), pipeline_mode=pl.Buffered(1)),
                  rows(d),
                  pl.BlockSpec((None, None, N_MOD, d), lambda i, j: (i, (j + tile0 >= nct).astype(jnp.int32), 0, 0)),
                  pl.BlockSpec(gains.shape, lambda i, j: (0, 0)),
                  pl.BlockSpec(w_router.shape, lambda i, j: (0, 0)),
                  pl.BlockSpec(w_router.shape, lambda i, j: (0, 0)),
                  pl.BlockSpec((1, ne), lambda i, j: (0, 0))],
        out_specs=[rows(d),
                   pl.BlockSpec((None, ROW_TILE, d // LANES, LANES), lambda i, j: (i, j + tile0, 0, 0)),
                   rows(TOP_K), rows(TOP_K)],
        compiler_params=_cparams("parallel", "parallel"),
        name="merge",
    )(h, *branch_ins, dn_gain_row, wg, wb, wo, x, modtab, gains, wr_hi, wr_lo, b_router.reshape(1, ne))


def _rank_kernel(idx_ref, rank_ref, count_ref, carry_ref, *, ne):
    @pl.when((pl.program_id(0) == 0) & (pl.program_id(1) == 0))
    def _():
        carry_ref[...] = jnp.zeros_like(carry_ref)

    idx = idx_ref[...]
    tm = idx.shape[0]
    lane = lax.broadcasted_iota(jnp.int32, (tm, ne), 1)
    onehots = [(lane == idx[:, kk:kk + 1]) for kk in range(TOP_K)]
    member = onehots[0] | onehots[1] | onehots[2] | onehots[3]
    i = lax.broadcasted_iota(jnp.int32, (tm, tm), 0)
    j = lax.broadcasted_iota(jnp.int32, (tm, tm), 1)
    before = _dot((j < i).astype(BF16), member.astype(BF16)) + carry_ref[...]
    for kk in range(TOP_K):
        rk = jnp.sum(jnp.where(onehots[kk], before, 0.0), axis=-1, keepdims=True)
        rank_ref[:, kk:kk + 1] = rk.astype(jnp.int32)
    carry_ref[...] = carry_ref[...] + jnp.sum(member.astype(F32), axis=0, keepdims=True)
    count_ref[...] = carry_ref[...].astype(jnp.int32)


def _moe_rank(top_idx, ne, tile0):
    b, t, _ = top_idx.shape
    rows = pl.BlockSpec((None, ROW_TILE, TOP_K), lambda i, j: (i, j + tile0, 0))
    return pl.pallas_call(
        functools.partial(_rank_kernel, ne=ne),
        out_shape=[jax.ShapeDtypeStruct((b, t, TOP_K), jnp.int32), jax.ShapeDtypeStruct((1, ne), jnp.int32)],
        grid=(b, t // ROW_TILE - tile0),
        in_specs=[rows],
        out_specs=[rows, pl.BlockSpec((1, ne), lambda i, j: (0, 0))],
        scratch_shapes=[pltpu.VMEM((1, ne), F32)],
        compiler_params=_cparams("arbitrary", "arbitrary"),
        name="moe_rank",
    )(top_idx)


def _row_copy(src, dst, sem):
    return pltpu.make_async_copy(src, dst, sem)


def _dispatch_kernel(dest_ref, h_ref, buf_in, buf_hbm, sem):
    del buf_in

    def issue(r, carry):
        for kk in range(TOP_K):
            _row_copy(h_ref.at[r], buf_hbm.at[dest_ref[0, r * TOP_K + kk]], sem).start()
        return carry

    lax.fori_loop(0, ROW_TILE, issue, 0)
    _row_copy(buf_hbm.at[pl.ds(0, ROW_TILE * TOP_K)], buf_hbm.at[pl.ds(0, ROW_TILE * TOP_K)], sem).wait()


def _moe_dispatch(h2, dest, n_rows, tile0):
    b, t, s, l = h2.shape
    tpb = t // ROW_TILE
    buf0 = jnp.zeros((n_rows, s, l), h2.dtype)
    dest2 = dest.reshape(b * tpb, 1, ROW_TILE * TOP_K)
    return pl.pallas_call(
        _dispatch_kernel,
        out_shape=jax.ShapeDtypeStruct((n_rows, s, l), h2.dtype),
        grid=(b, tpb - tile0),
        in_specs=[pl.BlockSpec((None, 1, ROW_TILE * TOP_K), lambda i, j: (i * tpb + j + tile0, 0, 0),
                               memory_space=pltpu.SMEM),
                  pl.BlockSpec((ROW_TILE, s, l), lambda i, j: (i * tpb + j + tile0, 0, 0)),
                  pl.BlockSpec(memory_space=pl.ANY)],
        out_specs=pl.BlockSpec(memory_space=pl.ANY),
        scratch_shapes=[pltpu.SemaphoreType.DMA],
        input_output_aliases={2: 0},
        compiler_params=pltpu.CompilerParams(dimension_semantics=("arbitrary", "arbitrary"),
                                             vmem_limit_bytes=VMEM_LIMIT, has_side_effects=True),
        name="moe_dispatch",
    )(dest2, h2.reshape(b * t, s, l), buf0)


def _ffn_kernel(te_ref, nu_ref, x_ref, wgu_ref, bgu_ref, wd_ref, bd_ref, y_ref):
    i = pl.program_id(0)
    ns = x_ref.shape[1]

    @pl.when(i < nu_ref[0])
    def _():
        subs = [pl.ds(h * MOE_SUB, MOE_SUB) for h in range(MOE_TILE // MOE_SUB)]
        gus = []
        for rows in subs:
            x = x_ref[rows].reshape(MOE_SUB, ns * LANES).astype(BF16)
            gus.append(_dot(x, wgu_ref[...]) + bgu_ref[...])
        for rows, gu in zip(subs, gus):
            f = gu.shape[1] // 2
            gate = jnp.minimum(gu[:, :f], SWIGLU_LIMIT)
            up = jnp.clip(gu[:, f:], -SWIGLU_LIMIT, SWIGLU_LIMIT)
            act = (up + 1.0) * gate * jax.nn.sigmoid(SWIGLU_ALPHA * gate)
            y = _dot(act.astype(BF16), wd_ref[...]) + bd_ref[...]
            y_ref[rows] = y.reshape(MOE_SUB, ns, LANES)

    @pl.when(i >= nu_ref[0])
    def _():
        y_ref[...] = jnp.zeros_like(y_ref)


def _moe_ffn(buf, tile_expert, n_used, wgu, bgu, wd, bd):
    n_rows, s, l = buf.shape
    ne, d, f2 = wgu.shape
    grid_spec = pltpu.PrefetchScalarGridSpec(
        num_scalar_prefetch=2,
        grid=(n_rows // MOE_TILE,),
        in_specs=[pl.BlockSpec((MOE_TILE, s, l), lambda i, te, nu: (i, 0, 0)),
                  pl.BlockSpec((None, d, f2), lambda i, te, nu: (te[i], 0, 0)),
                  pl.BlockSpec((None, 1, f2), lambda i, te, nu: (te[i], 0, 0)),
                  pl.BlockSpec((None, f2 // 2, d), lambda i, te, nu: (te[i], 0, 0)),
                  pl.BlockSpec((None, 1, d), lambda i, te, nu: (te[i], 0, 0))],
        out_specs=pl.BlockSpec((MOE_TILE, s, l), lambda i, te, nu: (i, 0, 0)),
    )
    return pl.pallas_call(
        _ffn_kernel,
        out_shape=jax.ShapeDtypeStruct((n_rows, s, l), F32),
        grid_spec=grid_spec,
        compiler_params=_cparams("arbitrary"),
        name="moe_ffn",
    )(tile_expert, n_used, buf, wgu, bgu.reshape(ne, 1, f2), wd, bd.reshape(ne, 1, d))


def _combine_kernel(dest_ref, next_ref, wgt_ref, y_hbm, x_ref, mod_ref, gain_ref, o_ref, gat_ref, acc_ref, sem):
    step = pl.program_id(0) * pl.num_programs(1) + pl.program_id(1)
    n_steps = pl.num_programs(0) * pl.num_programs(1)
    slot = step % 2

    def gather(idx_ref, to_slot):
        def issue(r, carry):
            for kk in range(TOP_K):
                _row_copy(y_hbm.at[idx_ref[0, r * TOP_K + kk]], gat_ref.at[to_slot, r * TOP_K + kk],
                          sem.at[to_slot]).start()
            return carry

        lax.fori_loop(0, ROW_TILE, issue, 0)

    @pl.when(step == 0)
    def _():
        gather(dest_ref, slot)

    @pl.when(step + 1 < n_steps)
    def _():
        gather(next_ref, 1 - slot)

    _row_copy(y_hbm.at[pl.ds(0, ROW_TILE * TOP_K)], gat_ref.at[slot], sem.at[slot]).wait()

    def mix(r, carry):
        acc = None
        for kk in range(TOP_K):
            term = wgt_ref[0, r * TOP_K + kk] * gat_ref[slot, r * TOP_K + kk]
            acc = term if acc is None else acc + term
        acc_ref[r] = acc
        return carry

    lax.fori_loop(0, ROW_TILE, mix, 0)
    f = acc_ref[...].reshape(x_ref.shape)
    o_ref[...] = x_ref[...] + mod_ref[5:6, :] * (_rms(f) * gain_ref[...])


def _moe_combine(ybuf, dest, wgt, x, modtab, gain_row, nct, tile0):
    b, t, d = x.shape
    tpb = t // ROW_TILE
    nt = tpb - tile0
    s, l = ybuf.shape[1:]
    dest2 = dest.reshape(b * tpb, 1, ROW_TILE * TOP_K)
    wgt2 = wgt.reshape(b * tpb, 1, ROW_TILE * TOP_K)
    tile_of = lambda i, j: i * tpb + j + tile0

    def next_tile(i, j):
        last = j == nt - 1
        return jnp.minimum(tile_of(jnp.where(last, i + 1, i), jnp.where(last, 0, j + 1)), b * tpb - 1)

    smem_rows = lambda f: pl.BlockSpec((None, 1, ROW_TILE * TOP_K), lambda i, j: (f(i, j), 0, 0),
                                       memory_space=pltpu.SMEM)
    return pl.pallas_call(
        _combine_kernel,
        out_shape=jax.ShapeDtypeStruct((b, t, d), F32),
        grid=(b, nt),
        in_specs=[smem_rows(tile_of), smem_rows(next_tile), smem_rows(tile_of),
                  pl.BlockSpec(memory_space=pl.ANY),
                  pl.BlockSpec((None, ROW_TILE, d), lambda i, j: (i, j + tile0, 0)),
                  pl.BlockSpec((None, None, N_MOD, d), lambda i, j: (i, (j + tile0 >= nct).astype(jnp.int32), 0, 0)),
                  pl.BlockSpec((1, d), lambda i, j: (0, 0))],
        out_specs=pl.BlockSpec((None, ROW_TILE, d), lambda i, j: (i, j + tile0, 0)),
        scratch_shapes=[pltpu.VMEM((2, ROW_TILE * TOP_K, s, l), F32),
                        pltpu.VMEM((ROW_TILE, s, l), F32),
                        pltpu.SemaphoreType.DMA((2,))],
        compiler_params=_cparams("arbitrary", "arbitrary"),
        name="moe_combine",
    )(dest2, dest2, wgt2, ybuf, x, modtab, gain_row)


def _deinterleave(width):
    idx = np.arange(width).reshape(-1, HEAD_DIM // 2, 2)
    return np.concatenate([idx[:, :, 0], idx[:, :, 1]], axis=1).reshape(-1)


def _rope_tables(ang, ctx_len):
    cos = jnp.cos(ang)
    sin = jnp.sin(ang)
    cos64 = jnp.concatenate([cos, cos], axis=-1)
    sin64 = jnp.concatenate([-sin, sin], axis=-1)
    reps = LANES // HEAD_DIM
    cos_t = jnp.concatenate([jnp.ones((ctx_len, HEAD_DIM), F32), cos64], axis=0)
    sin_t = jnp.concatenate([jnp.zeros((ctx_len, HEAD_DIM), F32), sin64], axis=0)
    return jnp.tile(cos_t, (1, reps)), jnp.tile(sin_t, (1, reps))


def _axial_angles(n_tok):
    rows = n_tok // GRID_W
    row = jnp.repeat(jnp.arange(rows), GRID_W).astype(F32)
    col = jnp.tile(jnp.arange(GRID_W), rows).astype(F32)
    half = HEAD_DIM // 2
    inv = ROPE_THETA ** (-jnp.arange(0, half, 2, dtype=F32) / half)
    return jnp.concatenate([row[:, None] * inv, col[:, None] * inv], axis=-1)


def _line_angles(n_tok):
    pos = jnp.arange(n_tok, dtype=F32)
    inv = ROPE_THETA ** (-jnp.arange(0, RET_QK_DIM, 2, dtype=F32) / RET_QK_DIM)
    return pos[:, None] * inv


def _layer(x, modtab, p, layer_idx, ctx_len, last):
    b, t, d = x.shape
    nct = ctx_len // ROW_TILE
    tile0 = nct if last else 0
    bwid = d // 2
    sizes = (bwid, GQA_KV_HEADS * HEAD_DIM, GQA_KV_HEADS * HEAD_DIM,
             bwid, bwid, bwid,
             RET_HEADS * RET_QK_DIM, RET_HEADS * RET_QK_DIM, bwid, bwid,
             3 * bwid, bwid, 2 * DN_HEADS, 2 * DN_HEADS,
             N_BRANCHES * d)
    cuts = np.concatenate([[0], np.cumsum(sizes)])
    w_in = p['w_in']
    col = lambda a, e: w_in[:, cuts[a]:cuts[e]]
    gains = p['norm_gain']
    ones_row = jnp.ones((1, LANES), F32)

    h = _modulate(x, modtab, gains[0:1], nct)

    seq = t - ctx_len
    cos_ax, sin_ax = _rope_tables(_axial_angles(seq), ctx_len)
    cos_ln, sin_ln = _rope_tables(_line_angles(seq), ctx_len)

    perm_q = _deinterleave(sizes[0])
    perm_k = _deinterleave(sizes[1])
    wa = jnp.concatenate([col(0, 1)[:, perm_q], col(1, 2)[:, perm_k], col(2, 3)], axis=1).astype(BF16)
    perm64 = _deinterleave(HEAD_DIM)
    qk_gain = p['gqa_qk_gain'].astype(F32)
    qscale = HEAD_DIM ** -0.5
    gain_a = jnp.concatenate([jnp.tile(qk_gain[0][perm64] * qscale, sizes[0] // HEAD_DIM),
                              jnp.tile(qk_gain[1][perm64], sizes[1] // HEAD_DIM)]).reshape(1, -1)
    qa, ka, va = _inproj(h, wa, cos_ax, sin_ax, gain_a,
                         [(sizes[0], BF16), (sizes[1], BF16), (sizes[2], BF16)], sizes[0] + sizes[1], True)
    oa = _gqa_attention(qa, ka, va, nct, ctx_len, tile0)

    perm_b = _deinterleave(bwid)
    wb_in = jnp.concatenate([col(3, 4)[:, perm_b] * qscale, col(4, 5)[:, perm_b], col(5, 6)], axis=1).astype(BF16)
    qb, kb, vb = _inproj(h, wb_in, cos_ax, sin_ax, ones_row,
                         [(bwid, BF16)] * 3, 2 * bwid, False)
    lam_init = 0.8 - 0.6 * math.exp(-0.3 * layer_idx)
    ob = _diff_attention(qb, kb, vb, p['diff_lambda'].astype(F32), p['diff_norm'].astype(F32).reshape(1, -1),
                         nct, ctx_len, tile0, lam_init)

    perm_c = _deinterleave(sizes[6])
    wc = jnp.concatenate([col(6, 7)[:, perm_c], col(7, 8)[:, perm_c] * (RET_QK_DIM ** -0.5), col(8, 10)],
                         axis=1).astype(BF16)
    qc, kc, vc, gc = _inproj(h, wc, cos_ln, sin_ln, ones_row,
                             [(sizes[6], BF16), (sizes[7], BF16), (bwid, BF16), (bwid, BF16)],
                             sizes[6] + sizes[7], False)
    log_gamma = jax.nn.log_sigmoid(p['ret_decay_logit'].astype(F32))
    oc_f, oc_b = _retention(qc, kc, vc, log_gamma, ctx_len)

    wd_in = col(10, 12).astype(BF16)
    w_ba = col(12, 14).astype(BF16)
    w_ba = jnp.pad(w_ba, ((0, 0), (0, LANES - w_ba.shape[1])))
    wd_all = jnp.concatenate([wd_in, w_ba], axis=1)
    qkv_d, z_d, ba_d = _inproj(h, wd_all, cos_ln, sin_ln, ones_row,
                               [(3 * bwid, F32), (bwid, BF16), (LANES, F32)], 0, False)
    nb = 2 * DN_HEADS
    rate_row = jnp.exp(p['dn_a_log'].astype(F32)).reshape(1, nb)
    bias_row = p['dn_dt_bias'].astype(F32).reshape(1, nb)
    qd, kd, vd, g_d, beta_d = _dn_prep(qkv_d, ba_d[:, :, :2 * nb], p['dn_conv_w'].astype(F32), rate_row, bias_row, nct)
    od_f, od_b = _dn_scan(*_dn_local(qd, kd, vd, g_d, beta_d), ctx_len)

    wg = col(14, 15).astype(BF16)
    x_new, h2, top_idx, top_w = _merge(h, (oa, ob, oc_f, oc_b, gc, od_f, od_b, z_d),
                                       p['dn_norm'].astype(F32).reshape(1, -1), wg, p['w_branch'].astype(BF16),
                                       p['w_out'].astype(BF16), x, modtab, gains,
                                       p['w_router'].astype(F32), p['b_router'].astype(F32), nct, tile0)

    ne = p['w_router'].shape[1]
    n_routed = b * (t - tile0 * ROW_TILE)
    rank, counts = _moe_rank(top_idx, ne, tile0)
    counts = counts.reshape(ne)
    padded = (counts + MOE_TILE - 1) // MOE_TILE * MOE_TILE
    pad_ends = jnp.cumsum(padded)
    pad_starts = pad_ends - padded
    experts = jnp.arange(ne, dtype=jnp.int32)
    dest = jnp.sum(jnp.where(top_idx[..., None] == experts, pad_starts.astype(jnp.int32), 0), axis=-1) + rank
    n_tiles = (n_routed * TOP_K) // MOE_TILE + ne
    tile_start = jnp.arange(n_tiles, dtype=jnp.int32) * MOE_TILE
    tile_expert = jnp.minimum(jnp.sum(pad_ends[None, :] <= tile_start[:, None], axis=1), ne - 1).astype(jnp.int32)
    n_used = (pad_ends[-1] // MOE_TILE).astype(jnp.int32).reshape(1)
    buf = _moe_dispatch(h2, dest, n_tiles * MOE_TILE, tile0)
    ybuf = _moe_ffn(buf, tile_expert, n_used, p['w_gate_up'].astype(BF16), p['b_gate_up'].astype(F32),
                    p['w_down'].astype(BF16), p['b_down'].astype(F32))
    return _moe_combine(ybuf, dest.reshape(b * t, TOP_K), top_w.reshape(b * t, TOP_K), x_new, modtab, gains[3:4],
                        nct, tile0)


def kernel(x, c, ctx, c_ctx, w_mod, b_mod, norm_gain, w_in, gqa_qk_gain, diff_lambda, diff_norm, ret_decay_logit,
           dn_conv_w, dn_a_log, dn_dt_bias, dn_norm, w_branch, w_out, w_router, b_router, w_gate_up, b_gate_up,
           w_down, b_down):
    b, seq, d = x.shape
    ctx_len = ctx.shape[1]
    depth = w_mod.shape[0]
    assert ctx_len % ROW_TILE == 0 and seq % ROW_TILE == 0 and seq % GRID_W == 0 and d % LANES == 0
    xa = jnp.concatenate([ctx, x], axis=1).astype(F32)
    c_all = jnp.concatenate([c, c_ctx[None, :]], axis=0).astype(F32)
    for l in range(depth):
        p = {
            'norm_gain': norm_gain[l].astype(F32), 'w_in': w_in[l], 'gqa_qk_gain': gqa_qk_gain[l],
            'diff_lambda': diff_lambda[l], 'diff_norm': diff_norm[l], 'ret_decay_logit': ret_decay_logit[l],
            'dn_conv_w': dn_conv_w[l], 'dn_a_log': dn_a_log[l], 'dn_dt_bias': dn_dt_bias[l], 'dn_norm': dn_norm[l],
            'w_branch': w_branch[l], 'w_out': w_out[l], 'w_router': w_router[l], 'b_router': b_router[l],
            'w_gate_up': w_gate_up[l], 'b_gate_up': b_gate_up[l], 'w_down': w_down[l], 'b_down': b_down[l],
        }
        mod = _mod_table(c_all, w_mod[l].astype(F32), b_mod[l].astype(F32))
        mod = mod.reshape(b + 1, N_MOD, d)
        modtab = jnp.stack([jnp.broadcast_to(mod[b], (b, N_MOD, d)), mod[:b]], axis=1)
        xa = _layer(xa, modtab, p, l, ctx_len, l == depth - 1)
    return xa[:, ctx_len:, :]
```

```python
import functools
import math

import numpy as np
import jax
import jax.numpy as jnp
from jax import lax
from jax.experimental import pallas as pl
from jax.experimental.pallas import tpu as pltpu

F32 = jnp.float32
BF16 = jnp.bfloat16

GRID_W = 64
RMS_EPS = 1e-6
ROPE_THETA = 10000.0
HEAD_DIM = 64
GQA_KV_HEADS = 2
GQA_GROUP = 4
DIFF_HEADS = 4
RET_HEADS = 4
RET_QK_DIM = 64
RET_V_DIM = 128
DN_HEADS = 4
DN_HEAD_DIM = 128
DN_CONV = 5
DN_CHUNK = 64
TOP_K = 4
SWIGLU_LIMIT = 7.0
SWIGLU_ALPHA = 1.702
N_MOD = 6
N_BRANCHES = 4

LANES = 128
SUBLANES = 8
ROW_TILE = 256
RET_CHUNK = 128
MERGE_SUB = 128
ATT_KEY_BLOCK = 256
SCAN_STEP_CHUNKS = 2
ATT_SKEW = 3
DN_BATCH_CHUNKS = 4
MOE_TILE = 512
MOE_SUB = 256
VMEM_LIMIT = 56 * 1024 * 1024


def _cparams(*sem):
    return pltpu.CompilerParams(dimension_semantics=sem, vmem_limit_bytes=VMEM_LIMIT)


def _const_spec(shape):
    nd = len(shape)
    return pl.BlockSpec(shape, lambda *_: (0,) * nd)


def _rms(x):
    return x * lax.rsqrt(jnp.mean(x * x, axis=-1, keepdims=True) + RMS_EPS)


def _silu(x):
    return x * jax.nn.sigmoid(x)


def _dot(a, b):
    return jnp.dot(a, b, preferred_element_type=F32)


def _dot_nt(a, b):
    return lax.dot_general(a, b, (((1,), (1,)), ((), ())), preferred_element_type=F32)


def _dot_tn(a, b):
    return lax.dot_general(a, b, (((0,), (0,)), ((), ())), preferred_element_type=F32)


def _mod_kernel(c_ref, w_ref, b_ref, o_ref):
    o_ref[...] = _dot(_silu(c_ref[...]), w_ref[...]) + b_ref[...]


def _mod_table(c_all, w_mod, b_mod):
    m, d = c_all.shape
    n = w_mod.shape[1]
    tn = d
    return pl.pallas_call(
        _mod_kernel,
        out_shape=jax.ShapeDtypeStruct((m, n), F32),
        grid=(n // tn,),
        in_specs=[pl.BlockSpec((m, d), lambda j: (0, 0)),
                  pl.BlockSpec((d, tn), lambda j: (0, j)),
                  pl.BlockSpec((1, tn), lambda j: (0, j))],
        out_specs=pl.BlockSpec((m, tn), lambda j: (0, j)),
        compiler_params=_cparams("parallel"),
        name="mod_table",
    )(c_all, w_mod, b_mod.reshape(1, n))


def _modulate_kernel(x_ref, mod_ref, gain_ref, h_ref):
    xn = _rms(x_ref[...]) * gain_ref[...]
    h = xn * (1.0 + mod_ref[1:2, :]) + mod_ref[0:1, :]
    h_ref[...] = h.astype(h_ref.dtype)


def _modulate(x, modtab, gain_row, nct):
    b, t, d = x.shape
    return pl.pallas_call(
        _modulate_kernel,
        out_shape=jax.ShapeDtypeStruct((b, t, d), BF16),
        grid=(b, t // ROW_TILE),
        in_specs=[pl.BlockSpec((None, ROW_TILE, d), lambda i, j: (i, j, 0)),
                  pl.BlockSpec((None, None, N_MOD, d), lambda i, j: (i, (j >= nct).astype(jnp.int32), 0, 0)),
                  pl.BlockSpec((1, d), lambda i, j: (0, 0))],
        out_specs=pl.BlockSpec((None, ROW_TILE, d), lambda i, j: (i, j, 0)),
        compiler_params=_cparams("parallel", "parallel"),
        name="modulate",
    )(x, modtab, gain_row)


def _swap_halves(y):
    lane = lax.broadcasted_iota(jnp.int32, y.shape, 1)
    first = (lane % HEAD_DIM) < (HEAD_DIM // 2)
    return jnp.where(first, pltpu.roll(y, LANES - HEAD_DIM // 2, 1), pltpu.roll(y, HEAD_DIM // 2, 1))


def _inproj_kernel(h_ref, w_ref, cos_ref, sin_ref, gain_ref, seg_ref, *out_refs, widths, n_rope, do_norm):
    h = h_ref[...]
    cos = cos_ref[...]
    sin = sin_ref[...]
    col = 0
    for o_ref, wd in zip(out_refs, widths):
        u = _dot(h, w_ref[:, col:col + wd])
        for j in range(0, wd, LANES):
            wj = min(LANES, wd - j)
            y = u[:, j: j + wj]
            if col + j < n_rope:
                if do_norm:
                    ms = _dot((y * y).astype(BF16), seg_ref[...])
                    y = y * lax.rsqrt(ms + RMS_EPS) * gain_ref[:, col + j: col + j + wj]
                y = y * cos + _swap_halves(y) * sin
            o_ref[:, j: j + wj] = y.astype(o_ref.dtype)
        col += wd


def _inproj(h, w, cos, sin, gain_row, outs, n_rope, do_norm):
    b, t, d = h.shape
    c = w.shape[1]
    widths = tuple(o[0] for o in outs)
    assert sum(widths) == c
    seg = np.kron(np.eye(LANES // HEAD_DIM), np.full((HEAD_DIM, HEAD_DIM), 1.0 / HEAD_DIM))
    seg = jnp.asarray(seg, BF16)
    kern = functools.partial(_inproj_kernel, widths=widths, n_rope=n_rope, do_norm=do_norm)
    return pl.pallas_call(
        kern,
        out_shape=[jax.ShapeDtypeStruct((b, t, wd), dt) for wd, dt in outs],
        grid=(b, t // ROW_TILE),
        in_specs=[pl.BlockSpec((None, ROW_TILE, d), lambda i, j: (i, j, 0)),
                  pl.BlockSpec((d, c), lambda i, j: (0, 0)),
                  pl.BlockSpec((ROW_TILE, LANES), lambda i, j: (j, 0)),
                  pl.BlockSpec((ROW_TILE, LANES), lambda i, j: (j, 0)),
                  pl.BlockSpec(gain_row.shape, lambda i, j: (0, 0)),
                  pl.BlockSpec((LANES, LANES), lambda i, j: (0, 0))],
        out_specs=[pl.BlockSpec((None, ROW_TILE, wd), lambda i, j: (i, j, 0)) for wd, _ in outs],
        compiler_params=_cparams("parallel", "parallel"),
        name="inproj",
    )(h, w, cos, sin, gain_row, seg)


def _softmax_attend(n_maps, nk, query, keys, values, emit, s_ref):
    kb = min(ATT_KEY_BLOCK, nk)
    blocks = [slice(j * kb, (j + 1) * kb) for j in range(nk // kb)]
    rows = s_ref.shape[1]
    lane_pieces = lambda a: [a[:, c:c + LANES] for c in range(0, kb, LANES)]

    def pass1_block(n, q, blk, m_run):
        s = _dot_nt(q, keys(n, blk))
        s_ref[n % 2, :, blk] = s
        for piece in lane_pieces(s):
            m_run = jnp.maximum(m_run, piece)
        return m_run

    neg = jnp.full((rows, LANES), -jnp.inf, F32)
    q_next = query(0)
    m_run = neg
    for blk in blocks:
        m_run = pass1_block(0, q_next, blk, m_run)
    for n in range(n_maps):
        m = jnp.max(m_run, axis=-1, keepdims=True)
        m_run = neg
        if n + 1 < n_maps:
            q_next = query(n + 1)
        l_run = jnp.zeros((rows, LANES), F32)
        acc = None
        ahead = blocks[:ATT_SKEW] if n + 1 < n_maps else []
        for blk in ahead:
            m_run = pass1_block(n + 1, q_next, blk, m_run)
        for idx, blk in enumerate(blocks):
            if n + 1 < n_maps and idx + ATT_SKEW < len(blocks):
                m_run = pass1_block(n + 1, q_next, blocks[idx + ATT_SKEW], m_run)
            p = jnp.exp(s_ref[n % 2, :, blk] - m)
            for piece in lane_pieces(p):
                l_run = l_run + piece
            pv = _dot(p.astype(BF16), values(n, blk))
            acc = pv if acc is None else acc + pv
        emit(n, acc / jnp.sum(l_run, axis=-1, keepdims=True))


def _gqa_kernel(q_ref, k_ref, v_ref, o_ref, s_ref, *, nct, ctx_len, tile0):
    t = pl.program_id(1) + tile0

    def run(nk):
        head = lambda n: slice(n * HEAD_DIM, (n + 1) * HEAD_DIM)
        kv_head = lambda n: head(n // GQA_GROUP)

        def emit(n, o):
            o_ref[:, head(n)] = o.astype(o_ref.dtype)

        _softmax_attend(GQA_KV_HEADS * GQA_GROUP, nk,
                        lambda n: q_ref[:, head(n)],
                        lambda n, blk: k_ref[blk, kv_head(n)],
                        lambda n, blk: v_ref[blk, kv_head(n)],
                        emit, s_ref)

    if tile0 < nct:
        @pl.when(t < nct)
        def _():
            run(ctx_len)

    @pl.when(t >= nct)
    def _():
        run(k_ref.shape[0])


def _gqa_attention(q, k, v, nct, ctx_len, tile0):
    b, t, cq = q.shape
    nt = t // ROW_TILE - tile0
    kern = functools.partial(_gqa_kernel, nct=nct, ctx_len=ctx_len, tile0=tile0)
    return pl.pallas_call(
        kern,
        out_shape=jax.ShapeDtypeStruct((b, t, cq), BF16),
        grid=(b, nt),
        in_specs=[pl.BlockSpec((None, ROW_TILE, cq), lambda i, j: (i, j + tile0, 0)),
                  pl.BlockSpec((None, t, k.shape[2]), lambda i, j: (i, 0, 0)),
                  pl.BlockSpec((None, t, v.shape[2]), lambda i, j: (i, 0, 0))],
        out_specs=pl.BlockSpec((None, ROW_TILE, cq), lambda i, j: (i, j + tile0, 0)),
        scratch_shapes=[pltpu.VMEM((2, ROW_TILE, t), F32)],
        compiler_params=_cparams("parallel", "parallel"),
        name="gqa_attention",
    )(q, k, v)


def _diff_kernel(q_ref, k_ref, v_ref, lam_ref, gain_ref, o_ref, s_ref, *, nct, ctx_len, tile0, lam_init):
    t = pl.program_id(1) + tile0
    lp = lam_ref[...]
    lam = (jnp.exp(jnp.sum(lp[0:1, :] * lp[1:2, :], axis=-1, keepdims=True))
           - jnp.exp(jnp.sum(lp[2:3, :] * lp[3:4, :], axis=-1, keepdims=True)) + lam_init)
    dv = 2 * HEAD_DIM

    def run(nk):
        qk_cols = lambda n: slice(n * HEAD_DIM, (n + 1) * HEAD_DIM)
        v_cols = lambda n: slice((n // 2) * dv, (n // 2 + 1) * dv)
        first = {}

        def emit(n, o):
            if n % 2 == 0:
                first[n // 2] = o
            else:
                out = _rms(first.pop(n // 2) - lam * o) * gain_ref[...] * (1.0 - lam_init)
                o_ref[:, v_cols(n)] = out.astype(o_ref.dtype)

        _softmax_attend(2 * DIFF_HEADS, nk,
                        lambda n: q_ref[:, qk_cols(n)],
                        lambda n, blk: k_ref[blk, qk_cols(n)],
                        lambda n, blk: v_ref[blk, v_cols(n)],
                        emit, s_ref)

    if tile0 < nct:
        @pl.when(t < nct)
        def _():
            run(ctx_len)

    @pl.when(t >= nct)
    def _():
        run(k_ref.shape[0])


def _diff_attention(q, k, v, lam_params, gain_row, nct, ctx_len, tile0, lam_init):
    b, t, cq = q.shape
    nt = t // ROW_TILE - tile0
    kern = functools.partial(_diff_kernel, nct=nct, ctx_len=ctx_len, tile0=tile0, lam_init=lam_init)
    return pl.pallas_call(
        kern,
        out_shape=jax.ShapeDtypeStruct((b, t, cq), BF16),
        grid=(b, nt),
        in_specs=[pl.BlockSpec((None, ROW_TILE, cq), lambda i, j: (i, j + tile0, 0)),
                  pl.BlockSpec((None, t, cq), lambda i, j: (i, 0, 0)),
                  pl.BlockSpec((None, t, cq), lambda i, j: (i, 0, 0)),
                  pl.BlockSpec(lam_params.shape, lambda i, j: (0, 0)),
                  pl.BlockSpec(gain_row.shape, lambda i, j: (0, 0))],
        out_specs=pl.BlockSpec((None, ROW_TILE, cq), lambda i, j: (i, j + tile0, 0)),
        scratch_shapes=[pltpu.VMEM((2, ROW_TILE, t), F32)],
        compiler_params=_cparams("parallel", "parallel"),
        name="diff_attention",
    )(q, k, v, lam_params, gain_row)


def _scan_chunk(d, s, n_ctx_chunks, n_chunks):
    rev = jnp.where(s < n_ctx_chunks, n_ctx_chunks - 1 - s, n_chunks + n_ctx_chunks - 1 - s)
    return jnp.where(d == 0, s, rev)


def _ret_kernel(lg_ref, qf_ref, kf_ref, vf_ref, qb_ref, kb_ref, vb_ref, of_ref, ob_ref, state_ref):
    c = RET_CHUNK
    nh = RET_HEADS

    @pl.when(pl.program_id(1) == 0)
    def _():
        state_ref[...] = jnp.zeros_like(state_ref)

    i = lax.broadcasted_iota(jnp.int32, (c, c), 0)
    j = lax.broadcasted_iota(jnp.int32, (c, c), 1)
    r = lax.broadcasted_iota(jnp.int32, (c, 1), 0)
    lg = jnp.stack([jnp.full((1, 1), lg_ref[d, hd], F32) for d in range(2) for hd in range(nh)])
    rel = jnp.stack([i - j] * nh + [j - i] * nh)
    intra = jnp.where(rel >= 0, jnp.exp(lg * jnp.maximum(rel, 0).astype(F32)), 0.0)
    pos_q = jnp.stack([r + 1] * nh + [c - r] * nh).astype(F32)
    pos_k = jnp.stack([c - 1 - r] * nh + [r] * nh).astype(F32)

    def stacked(refs, width):
        return jnp.stack([ref[:, hd * width:(hd + 1) * width] for ref in refs for hd in range(nh)])

    q = stacked((qf_ref, qb_ref), RET_QK_DIM)
    k = stacked((kf_ref, kb_ref), RET_QK_DIM)
    v = stacked((vf_ref, vb_ref), RET_V_DIM)
    st = state_ref[...]
    scores = _bdot_nt(q, k) * intra
    qd = (q.astype(F32) * jnp.exp(lg * pos_q)).astype(BF16)
    o = _bdot(scores.astype(BF16), v) + _bdot(qd, st.astype(BF16))
    kd = (k.astype(F32) * jnp.exp(lg * pos_k)).astype(BF16)
    state_ref[...] = st * jnp.exp(lg * c) + _bdot_tn(kd, v)
    for n in range(2 * nh):
        d, hd = divmod(n, nh)
        (of_ref, ob_ref)[d][:, hd * RET_V_DIM:(hd + 1) * RET_V_DIM] = o[n]


def _retention(q, k, v, log_gamma, ctx_len):
    b, t, _ = q.shape
    n = t // RET_CHUNK
    nc = ctx_len // RET_CHUNK

    def specs(d):
        rows = lambda a: pl.BlockSpec((None, RET_CHUNK, a.shape[2]), lambda i, s: (i, _scan_chunk(d, s, nc, n), 0))
        return [rows(q), rows(k), rows(v)]

    return pl.pallas_call(
        _ret_kernel,
        out_shape=[jax.ShapeDtypeStruct((b, t, v.shape[2]), F32)] * 2,
        grid=(b, n),
        in_specs=[pl.BlockSpec(memory_space=pltpu.SMEM)] + specs(0) + specs(1),
        out_specs=[specs(0)[2], specs(1)[2]],
        scratch_shapes=[pltpu.VMEM((2 * RET_HEADS, RET_QK_DIM, RET_V_DIM), F32)],
        compiler_params=_cparams("parallel", "arbitrary"),
        name="retention_scan",
    )(log_gamma, q, k, v, q, k, v)


def _dn_prep_kernel(prev_ref, cur_ref, next_ref, cw_ref, ba_ref, rate_ref, bias_ref,
                    q_ref, k_ref, v_ref, g_ref, beta_ref, *, nct, n_tiles):
    t = pl.program_id(1)
    pad = DN_CONV // 2
    first = (t == 0) | (t == nct)
    last = (t == nct - 1) | (t == n_tiles - 1)
    prev = jnp.where(first, 0.0, prev_ref[...])
    nxt = jnp.where(last, 0.0, next_ref[...])
    ext = jnp.concatenate([prev, cur_ref[...], nxt], axis=0)
    rows = ext.shape[0]
    acc = None
    for tap in range(DN_CONV):
        sh = (pad - tap) % rows
        x = ext if sh == 0 else pltpu.roll(ext, sh, 0)
        term = x[SUBLANES:SUBLANES + ROW_TILE, :] * cw_ref[tap:tap + 1, :]
        acc = term if acc is None else acc + term
    y = _silu(acc)
    hw = DN_HEADS * DN_HEAD_DIM
    for hd in range(DN_HEADS):
        sl = slice(hd * DN_HEAD_DIM, (hd + 1) * DN_HEAD_DIM)
        qh = y[:, sl]
        kh = y[:, hw + hd * DN_HEAD_DIM: hw + (hd + 1) * DN_HEAD_DIM]
        q_ref[:, sl] = qh * lax.rsqrt(jnp.sum(qh * qh, axis=-1, keepdims=True) + RMS_EPS) * (DN_HEAD_DIM ** -0.5)
        k_ref[:, sl] = kh * lax.rsqrt(jnp.sum(kh * kh, axis=-1, keepdims=True) + RMS_EPS)
    v_ref[...] = y[:, 2 * hw:3 * hw]
    nb = 2 * DN_HEADS
    ba = ba_ref[...]
    beta_ref[...] = jax.nn.sigmoid(ba[:, 0:nb])
    a = ba[:, nb:2 * nb] + bias_ref[...]
    softplus = jnp.maximum(a, 0.0) + jnp.log(1.0 + jnp.exp(-jnp.abs(a)))
    g_ref[...] = -rate_ref[...] * softplus


def _dn_prep(qkv, ba, conv_w, rate_row, bias_row, nct):
    b, t, c = qkv.shape
    n_tiles = t // ROW_TILE
    hb = ROW_TILE // SUBLANES
    last_blk = t // SUBLANES - 1
    hw = DN_HEADS * DN_HEAD_DIM
    nb = 2 * DN_HEADS
    kern = functools.partial(_dn_prep_kernel, nct=nct, n_tiles=n_tiles)
    row_spec = lambda wd: pl.BlockSpec((None, ROW_TILE, wd), lambda i, j: (i, j, 0))
    return pl.pallas_call(
        kern,
        out_shape=[jax.ShapeDtypeStruct((b, t, hw), F32)] * 3 + [jax.ShapeDtypeStruct((b, t, nb), F32)] * 2,
        grid=(b, n_tiles),
        in_specs=[pl.BlockSpec((None, SUBLANES, c), lambda i, j: (i, jnp.maximum(j * hb - 1, 0), 0)),
                  row_spec(c),
                  pl.BlockSpec((None, SUBLANES, c), lambda i, j: (i, jnp.minimum((j + 1) * hb, last_blk), 0)),
                  pl.BlockSpec(conv_w.shape, lambda i, j: (0, 0)),
                  row_spec(2 * nb),
                  pl.BlockSpec((1, nb), lambda i, j: (0, 0)),
                  pl.BlockSpec((1, nb), lambda i, j: (0, 0))],
        out_specs=[row_spec(hw)] * 3 + [row_spec(nb)] * 2,
        compiler_params=_cparams("parallel", "parallel"),
        name="deltanet_prep",
    )(qkv, qkv, qkv, conv_w, ba, rate_row, bias_row)


def _split2(x):
    hi = x.astype(BF16)
    lo = (x - hi.astype(F32)).astype(BF16)
    return hi, lo


def _bdot(a, b):
    return lax.dot_general(a, b, (((2,), (1,)), ((0,), (0,))), preferred_element_type=F32)


def _bdot_nt(a, b):
    return lax.dot_general(a, b, (((2,), (2,)), ((0,), (0,))), preferred_element_type=F32)


def _bdot_tn(a, b):
    return lax.dot_general(a, b, (((1,), (1,)), ((0,), (0,))), preferred_element_type=F32)


def _bdot_bf(a, b):
    return _bdot(a.astype(BF16), b.astype(BF16))


def _bdot_01(a01_bf16, x):
    xh, xl = _split2(x)
    return _bdot(a01_bf16, xh) + _bdot(a01_bf16, xl)


def _dn_local_kernel(q_ref, k_ref, v_ref, g_ref, beta_ref, u_ref, w_ref, qg_ref, kd_ref, attn_ref, el_ref):
    c = DN_CHUNK
    dh = DN_HEAD_DIM
    i = lax.broadcasted_iota(jnp.int32, (c, c), 0)
    j = lax.broadcasted_iota(jnp.int32, (c, c), 1)
    eye = (i == j).astype(F32)

    nh = DN_HEADS
    nu = 2 * nh * DN_BATCH_CHUNKS
    rel = jnp.stack(([i - j] * nh + [j - i] * nh) * DN_BATCH_CHUNKS)
    incl = rel >= 0
    strict = rel > 0
    tri = incl.astype(BF16)

    def chunk(ci, carry):
        r0 = pl.multiple_of(ci * (c * DN_BATCH_CHUNKS), c * DN_BATCH_CHUNKS)
        rows_of = [pl.ds(r0 + m * c, c) for m in range(DN_BATCH_CHUNKS)]
        g_all = [g_ref[rows, :] for rows in rows_of]
        b_all = [beta_ref[rows, :] for rows in rows_of]
        g = jnp.stack([ga[:, n:n + 1] for ga in g_all for n in range(2 * nh)])
        beta = jnp.stack([ba[:, n:n + 1] for ba in b_all for n in range(2 * nh)])
        heads = lambda ref: jnp.stack([ref[rows, hd * dh:(hd + 1) * dh]
                                       for rows in rows_of for _ in range(2) for hd in range(nh)])
        q = heads(q_ref)
        k = heads(k_ref)
        v = heads(v_ref)
        gc = _bdot_01(tri, jnp.broadcast_to(g, (nu, c, dh)))
        dmat = _bdot_01(tri, jnp.where(strict, jnp.broadcast_to(g, (nu, c, c)), 0.0))
        decay = jnp.where(incl, jnp.exp(dmat), 0.0)
        kb = k * beta
        kbf = k.astype(BF16)
        a = jnp.where(strict, -(_bdot_nt(kb.astype(BF16), kbf) * decay), 0.0)
        tinv = eye + a
        p = a
        for _ in range(5):
            pb = p.astype(BF16)
            p = _bdot(pb, pb)
            tinv = tinv + _bdot_bf(tinv, p)
        egc = jnp.exp(gc)
        tb = tinv.astype(BF16)
        u = _bdot(tb, (v * beta).astype(BF16))
        w = _bdot(tb, (kb * egc).astype(BF16))
        attn = jnp.where(incl, _bdot_nt(q.astype(BF16), kbf) * decay, 0.0)
        ends = [c - 1 if (n // nh) % 2 == 0 else 0 for n in range(nu)]
        last = jnp.concatenate([gc[n:n + 1, e:e + 1, :] for n, e in enumerate(ends)], axis=0)
        kd = k * jnp.exp(last - gc)
        qg = q * egc
        el = jnp.exp(last)
        for n in range(nu):
            m, rest = divmod(n, 2 * nh)
            d, hd = divmod(rest, nh)
            rows = rows_of[m]
            sl = slice(hd * dh, (hd + 1) * dh)
            u_ref[d, rows, sl] = u[n]
            w_ref[d, rows, sl] = w[n].astype(w_ref.dtype)
            attn_ref[d, rows, hd * c:(hd + 1) * c] = attn[n].astype(attn_ref.dtype)
            kd_ref[d, rows, sl] = kd[n].astype(kd_ref.dtype)
            qg_ref[d, rows, sl] = qg[n].astype(qg_ref.dtype)
            el_ref[d, ci * DN_BATCH_CHUNKS + m, :, sl] = el[n]
        return carry

    lax.fori_loop(0, ROW_TILE // (c * DN_BATCH_CHUNKS), chunk, 0)


def _dn_local(q, k, v, g, beta):
    b, t, hw = q.shape
    cpt = ROW_TILE // DN_CHUNK
    rows = lambda wd: pl.BlockSpec((None, ROW_TILE, wd), lambda i, j: (i, j, 0))
    rows2 = lambda wd: pl.BlockSpec((2, None, ROW_TILE, wd), lambda i, j: (0, i, j, 0))
    aw = DN_HEADS * DN_CHUNK
    return pl.pallas_call(
        _dn_local_kernel,
        out_shape=[jax.ShapeDtypeStruct((2, b, t, hw), F32),
                   jax.ShapeDtypeStruct((2, b, t, hw), BF16),
                   jax.ShapeDtypeStruct((2, b, t, hw), BF16),
                   jax.ShapeDtypeStruct((2, b, t, hw), BF16),
                   jax.ShapeDtypeStruct((2, b, t, aw), BF16),
                   jax.ShapeDtypeStruct((2, b, t // DN_CHUNK, 1, hw), F32)],
        grid=(b, t // ROW_TILE),
        in_specs=[rows(hw), rows(hw), rows(hw), rows(g.shape[2]), rows(g.shape[2])],
        out_specs=[rows2(hw), rows2(hw), rows2(hw), rows2(hw), rows2(aw),
                   pl.BlockSpec((2, None, cpt, 1, hw), lambda i, j: (0, i, j, 0, 0))],
        compiler_params=_cparams("parallel", "parallel"),
        name="deltanet_local",
    )(q, k, v, g, beta)


def _dn_scan_kernel(*refs):
    ins = refs[:12]
    of_ref, ob_ref, state_ref = refs[12:]
    dh = DN_HEAD_DIM
    c = DN_CHUNK

    @pl.when(pl.program_id(1) == 0)
    def _():
        state_ref[...] = jnp.zeros_like(state_ref)

    nh = DN_HEADS

    st = state_ref[...]
    for m in range(SCAN_STEP_CHUNKS):
        sub = [m, SCAN_STEP_CHUNKS - 1 - m]
        rows = [pl.ds(sub[d] * c, c) for d in range(2)]

        def stacked(k, width):
            return jnp.stack([ins[6 * d + k][rows[d], hd * width:(hd + 1) * width]
                              for d in range(2) for hd in range(nh)])

        el = jnp.stack([ins[6 * d + 5][sub[d], :, hd * dh:(hd + 1) * dh] for d in range(2) for hd in range(nh)])
        stb = st.astype(BF16)
        v_new = stacked(0, dh) - _bdot(stacked(1, dh), stb)
        vnb = v_new.astype(BF16)
        o = _bdot(stacked(2, dh), stb) + _bdot(stacked(4, c), vnb)
        st = st * el + _bdot_tn(stacked(3, dh), vnb)
        for n in range(2 * nh):
            d, hd = divmod(n, nh)
            (of_ref, ob_ref)[d][rows[d], hd * dh:(hd + 1) * dh] = o[n]
    state_ref[...] = st


def _dn_scan(u, w, qg, kd, attn, el, ctx_len):
    _, b, t, hw = u.shape
    blk = DN_CHUNK * SCAN_STEP_CHUNKS
    assert ctx_len % blk == 0 and t % blk == 0
    n = t // blk
    nc = ctx_len // blk
    aw = attn.shape[3]

    def specs(d):
        rows = lambda wd: pl.BlockSpec((None, None, blk, wd), lambda i, s: (d, i, _scan_chunk(d, s, nc, n), 0))
        return [rows(hw), rows(hw), rows(hw), rows(hw), rows(aw),
                pl.BlockSpec((None, None, SCAN_STEP_CHUNKS, 1, hw),
                             lambda i, s: (d, i, _scan_chunk(d, s, nc, n), 0, 0))]

    out = lambda d: pl.BlockSpec((None, blk, hw), lambda i, s: (i, _scan_chunk(d, s, nc, n), 0))
    args = (u, w, qg, kd, attn, el)
    return pl.pallas_call(
        _dn_scan_kernel,
        out_shape=[jax.ShapeDtypeStruct((b, t, hw), F32)] * 2,
        grid=(b, n),
        in_specs=specs(0) + specs(1),
        out_specs=[out(0), out(1)],
        scratch_shapes=[pltpu.VMEM((2 * DN_HEADS, DN_HEAD_DIM, DN_HEAD_DIM), F32)],
        compiler_params=_cparams("parallel", "arbitrary"),
        name="deltanet_scan",
    )(*args, *args)


def _gated_heads(o_fwd, o_bwd, z, gain):
    o = o_fwd + o_bwd
    pieces = []
    for j in range(0, o.shape[1], LANES):
        oh = _rms(o[:, j:j + LANES]) * gain
        pieces.append((oh * _silu(z[:, j:j + LANES].astype(F32))).astype(BF16))
    return jnp.concatenate(pieces, axis=-1)


def _merge_kernel(h_ref, oa_ref, ob_ref, ocf_ref, ocb_ref, gc_ref, odf_ref, odb_ref, zd_ref, dng_ref,
                  wg_ref, wb_ref, wo_ref, x_ref, mod_ref, gains_ref,
                  wrh_ref, wrl_ref, br_ref, xn_ref, h2_ref, idx_ref, wgt_ref):
    d = x_ref.shape[1]
    subs = [pl.ds(n * MERGE_SUB, MERGE_SUB) for n in range(ROW_TILE // MERGE_SUB)]
    ones = jnp.ones((1, LANES), F32)
    ys = []
    for rows in subs:
        h = h_ref[rows, :]
        branch = (lambda: oa_ref[rows, :],
                  lambda: ob_ref[rows, :],
                  lambda: _gated_heads(ocf_ref[rows, :], ocb_ref[rows, :], gc_ref[rows, :], ones),
                  lambda: _gated_heads(odf_ref[rows, :], odb_ref[rows, :], zd_ref[rows, :], dng_ref[...]))
        merged = None
        for n, o_of in enumerate(branch):
            gate = jax.nn.sigmoid(_dot(h, wg_ref[:, n * d:(n + 1) * d]))
            term = gate * _dot(o_of(), wb_ref[n])
            merged = term if merged is None else merged + term
        ys.append(_dot(merged.astype(BF16), wo_ref[...]))
    for rows, y in zip(subs, ys):
        x_new = x_ref[rows, :] + mod_ref[2:3, :] * (_rms(y) * gains_ref[1:2, :])
        xn_ref[rows, :] = x_new
        h2 = _rms(x_new) * gains_ref[2:3, :] * (1.0 + mod_ref[4:5, :]) + mod_ref[3:4, :]
        h2_ref[rows] = h2.reshape((MERGE_SUB,) + h2_ref.shape[1:])
        h2h, h2l = _split2(h2)
        logits = _dot(h2h, wrh_ref[...]) + _dot(h2l, wrh_ref[...]) + _dot(h2h, wrl_ref[...]) + br_ref[...]
        ne = logits.shape[1]
        lane = lax.broadcasted_iota(jnp.int32, logits.shape, 1)
        vals = []
        for kk in range(TOP_K):
            m = jnp.max(logits, axis=-1, keepdims=True)
            sel = jnp.min(jnp.where(logits == m, lane, ne), axis=-1, keepdims=True)
            sel = jnp.minimum(sel, ne - 1)
            idx_ref[rows, kk:kk + 1] = sel
            vals.append(m)
            logits = jnp.where(lane == sel, -jnp.inf, logits)
        es = [jnp.exp(vv - vals[0]) for vv in vals]
        tot = es[0] + es[1] + es[2] + es[3]
        for kk in range(TOP_K):
            wgt_ref[rows, kk:kk + 1] = es[kk] / tot


def _merge(h, branch_ins, dn_gain_row, wg, wb, wo, x, modtab, gains, w_router, b_router, nct, tile0):
    b, t, d = x.shape
    nt = t // ROW_TILE - tile0
    ne = w_router.shape[1]
    rows = lambda wd: pl.BlockSpec((None, ROW_TILE, wd), lambda i, j: (i, j + tile0, 0))
    bw = branch_ins[0].shape[2]
    wr_hi = w_router.astype(BF16)
    wr_lo = (w_router - wr_hi.astype(F32)).astype(BF16)
    return pl.pallas_call(
        _merge_kernel,
        out_shape=[jax.ShapeDtypeStruct((b, t, d), F32),
                   jax.ShapeDtypeStruct((b, t, d // LANES, LANES), F32),
                   jax.ShapeDtypeStruct((b, t, TOP_K), jnp.int32),
                   jax.ShapeDtypeStruct((b, t, TOP_K), F32)],
        grid=(b, nt),
        in_specs=[rows(d)] + [rows(bw)] * len(branch_ins)
                 + [pl.BlockSpec((1, LANES), lambda i, j: (0, 0)),
                  pl.BlockSpec(wg.shape, lambda i, j: (0, 0), pipeline_mode=pl.Buffered(1)),
                  pl.BlockSpec(wb.shape, lambda i, j: (0, 0, 0), pipeline_mode=pl.Buffered(1)),
                  pl.BlockSpec(wo.shape, lambda i, j: (0, 0), pipeline_mode=pl.Buffered(1)),
                  rows(d),
                  pl.BlockSpec((None, None, N_MOD, d), lambda i, j: (i, (j + tile0 >= nct).astype(jnp.int32), 0, 0)),
                  pl.BlockSpec(gains.shape, lambda i, j: (0, 0)),
                  pl.BlockSpec(w_router.shape, lambda i, j: (0, 0)),
                  pl.BlockSpec(w_router.shape, lambda i, j: (0, 0)),
                  pl.BlockSpec((1, ne), lambda i, j: (0, 0))],
        out_specs=[rows(d),
                   pl.BlockSpec((None, ROW_TILE, d // LANES, LANES), lambda i, j: (i, j + tile0, 0, 0)),
                   rows(TOP_K), rows(TOP_K)],
        compiler_params=_cparams("parallel", "parallel"),
        name="merge",
    )(h, *branch_ins, dn_gain_row, wg, wb, wo, x, modtab, gains, wr_hi, wr_lo, b_router.reshape(1, ne))


def _rank_kernel(idx_ref, rank_ref, count_ref, carry_ref, *, ne):
    @pl.when((pl.program_id(0) == 0) & (pl.program_id(1) == 0))
    def _():
        carry_ref[...] = jnp.zeros_like(carry_ref)

    idx = idx_ref[...]
    tm = idx.shape[0]
    lane = lax.broadcasted_iota(jnp.int32, (tm, ne), 1)
    onehots = [(lane == idx[:, kk:kk + 1]) for kk in range(TOP_K)]
    member = onehots[0] | onehots[1] | onehots[2] | onehots[3]
    i = lax.broadcasted_iota(jnp.int32, (tm, tm), 0)
    j = lax.broadcasted_iota(jnp.int32, (tm, tm), 1)
    before = _dot((j < i).astype(BF16), member.astype(BF16)) + carry_ref[...]
    for kk in range(TOP_K):
        rk = jnp.sum(jnp.where(onehots[kk], before, 0.0), axis=-1, keepdims=True)
        rank_ref[:, kk:kk + 1] = rk.astype(jnp.int32)
    carry_ref[...] = carry_ref[...] + jnp.sum(member.astype(F32), axis=0, keepdims=True)
    count_ref[...] = carry_ref[...].astype(jnp.int32)


def _moe_rank(top_idx, ne, tile0):
    b, t, _ = top_idx.shape
    rows = pl.BlockSpec((None, ROW_TILE, TOP_K), lambda i, j: (i, j + tile0, 0))
    return pl.pallas_call(
        functools.partial(_rank_kernel, ne=ne),
        out_shape=[jax.ShapeDtypeStruct((b, t, TOP_K), jnp.int32), jax.ShapeDtypeStruct((1, ne), jnp.int32)],
        grid=(b, t // ROW_TILE - tile0),
        in_specs=[rows],
        out_specs=[rows, pl.BlockSpec((1, ne), lambda i, j: (0, 0))],
        scratch_shapes=[pltpu.VMEM((1, ne), F32)],
        compiler_params=_cparams("arbitrary", "arbitrary"),
        name="moe_rank",
    )(top_idx)


def _row_copy(src, dst, sem):
    return pltpu.make_async_copy(src, dst, sem)


def _dispatch_kernel(dest_ref, h_ref, buf_in, buf_hbm, sem):
    del buf_in

    def issue(r, carry):
        for kk in range(TOP_K):
            _row_copy(h_ref.at[r], buf_hbm.at[dest_ref[0, r * TOP_K + kk]], sem).start()
        return carry

    lax.fori_loop(0, ROW_TILE, issue, 0)
    _row_copy(buf_hbm.at[pl.ds(0, ROW_TILE * TOP_K)], buf_hbm.at[pl.ds(0, ROW_TILE * TOP_K)], sem).wait()


def _moe_dispatch(h2, dest, n_rows, tile0):
    b, t, s, l = h2.shape
    tpb = t // ROW_TILE
    buf0 = jnp.zeros((n_rows, s, l), h2.dtype)
    dest2 = dest.reshape(b * tpb, 1, ROW_TILE * TOP_K)
    return pl.pallas_call(
        _dispatch_kernel,
        out_shape=jax.ShapeDtypeStruct((n_rows, s, l), h2.dtype),
        grid=(b, tpb - tile0),
        in_specs=[pl.BlockSpec((None, 1, ROW_TILE * TOP_K), lambda i, j: (i * tpb + j + tile0, 0, 0),
                               memory_space=pltpu.SMEM),
                  pl.BlockSpec((ROW_TILE, s, l), lambda i, j: (i * tpb + j + tile0, 0, 0)),
                  pl.BlockSpec(memory_space=pl.ANY)],
        out_specs=pl.BlockSpec(memory_space=pl.ANY),
        scratch_shapes=[pltpu.SemaphoreType.DMA],
        input_output_aliases={2: 0},
        compiler_params=pltpu.CompilerParams(dimension_semantics=("arbitrary", "arbitrary"),
                                             vmem_limit_bytes=VMEM_LIMIT, has_side_effects=True),
        name="moe_dispatch",
    )(dest2, h2.reshape(b * t, s, l), buf0)


def _ffn_kernel(te_ref, nu_ref, x_ref, wgu_ref, bgu_ref, wd_ref, bd_ref, y_ref):
    i = pl.program_id(0)
    ns = x_ref.shape[1]

    @pl.when(i < nu_ref[0])
    def _():
        subs = [pl.ds(h * MOE_SUB, MOE_SUB) for h in range(MOE_TILE // MOE_SUB)]
        gus = []
        for rows in subs:
            x = x_ref[rows].reshape(MOE_SUB, ns * LANES).astype(BF16)
            gus.append(_dot(x, wgu_ref[...]) + bgu_ref[...])
        for rows, gu in zip(subs, gus):
            f = gu.shape[1] // 2
            gate = jnp.minimum(gu[:, :f], SWIGLU_LIMIT)
            up = jnp.clip(gu[:, f:], -SWIGLU_LIMIT, SWIGLU_LIMIT)
            act = (up + 1.0) * gate * jax.nn.sigmoid(SWIGLU_ALPHA * gate)
            y = _dot(act.astype(BF16), wd_ref[...]) + bd_ref[...]
            y_ref[rows] = y.reshape(MOE_SUB, ns, LANES)

    @pl.when(i >= nu_ref[0])
    def _():
        y_ref[...] = jnp.zeros_like(y_ref)


def _moe_ffn(buf, tile_expert, n_used, wgu, bgu, wd, bd):
    n_rows, s, l = buf.shape
    ne, d, f2 = wgu.shape
    grid_spec = pltpu.PrefetchScalarGridSpec(
        num_scalar_prefetch=2,
        grid=(n_rows // MOE_TILE,),
        in_specs=[pl.BlockSpec((MOE_TILE, s, l), lambda i, te, nu: (i, 0, 0)),
                  pl.BlockSpec((None, d, f2), lambda i, te, nu: (te[i], 0, 0)),
                  pl.BlockSpec((None, 1, f2), lambda i, te, nu: (te[i], 0, 0)),
                  pl.BlockSpec((None, f2 // 2, d), lambda i, te, nu: (te[i], 0, 0)),
                  pl.BlockSpec((None, 1, d), lambda i, te, nu: (te[i], 0, 0))],
        out_specs=pl.BlockSpec((MOE_TILE, s, l), lambda i, te, nu: (i, 0, 0)),
    )
    return pl.pallas_call(
        _ffn_kernel,
        out_shape=jax.ShapeDtypeStruct((n_rows, s, l), F32),
        grid_spec=grid_spec,
        compiler_params=_cparams("arbitrary"),
        name="moe_ffn",
    )(tile_expert, n_used, buf, wgu, bgu.reshape(ne, 1, f2), wd, bd.reshape(ne, 1, d))


def _combine_kernel(dest_ref, next_ref, wgt_ref, y_hbm, x_ref, mod_ref, gain_ref, o_ref, gat_ref, acc_ref, sem):
    step = pl.program_id(0) * pl.num_programs(1) + pl.program_id(1)
    n_steps = pl.num_programs(0) * pl.num_programs(1)
    slot = step % 2

    def gather(idx_ref, to_slot):
        def issue(r, carry):
            for kk in range(TOP_K):
                _row_copy(y_hbm.at[idx_ref[0, r * TOP_K + kk]], gat_ref.at[to_slot, r * TOP_K + kk],
                          sem.at[to_slot]).start()
            return carry

        lax.fori_loop(0, ROW_TILE, issue, 0)

    @pl.when(step == 0)
    def _():
        gather(dest_ref, slot)

    @pl.when(step + 1 < n_steps)
    def _():
        gather(next_ref, 1 - slot)

    _row_copy(y_hbm.at[pl.ds(0, ROW_TILE * TOP_K)], gat_ref.at[slot], sem.at[slot]).wait()

    def mix(r, carry):
        acc = None
        for kk in range(TOP_K):
            term = wgt_ref[0, r * TOP_K + kk] * gat_ref[slot, r * TOP_K + kk]
            acc = term if acc is None else acc + term
        acc_ref[r] = acc
        return carry

    lax.fori_loop(0, ROW_TILE, mix, 0)
    f = acc_ref[...].reshape(x_ref.shape)
    o_ref[...] = x_ref[...] + mod_ref[5:6, :] * (_rms(f) * gain_ref[...])


def _moe_combine(ybuf, dest, wgt, x, modtab, gain_row, nct, tile0):
    b, t, d = x.shape
    tpb = t // ROW_TILE
    nt = tpb - tile0
    s, l = ybuf.shape[1:]
    dest2 = dest.reshape(b * tpb, 1, ROW_TILE * TOP_K)
    wgt2 = wgt.reshape(b * tpb, 1, ROW_TILE * TOP_K)
    tile_of = lambda i, j: i * tpb + j + tile0

    def next_tile(i, j):
        last = j == nt - 1
        return jnp.minimum(tile_of(jnp.where(last, i + 1, i), jnp.where(last, 0, j + 1)), b * tpb - 1)

    smem_rows = lambda f: pl.BlockSpec((None, 1, ROW_TILE * TOP_K), lambda i, j: (f(i, j), 0, 0),
                                       memory_space=pltpu.SMEM)
    return pl.pallas_call(
        _combine_kernel,
        out_shape=jax.ShapeDtypeStruct((b, t, d), F32),
        grid=(b, nt),
        in_specs=[smem_rows(tile_of), smem_rows(next_tile), smem_rows(tile_of),
                  pl.BlockSpec(memory_space=pl.ANY),
                  pl.BlockSpec((None, ROW_TILE, d), lambda i, j: (i, j + tile0, 0)),
                  pl.BlockSpec((None, None, N_MOD, d), lambda i, j: (i, (j + tile0 >= nct).astype(jnp.int32), 0, 0)),
                  pl.BlockSpec((1, d), lambda i, j: (0, 0))],
        out_specs=pl.BlockSpec((None, ROW_TILE, d), lambda i, j: (i, j + tile0, 0)),
        scratch_shapes=[pltpu.VMEM((2, ROW_TILE * TOP_K, s, l), F32),
                        pltpu.VMEM((ROW_TILE, s, l), F32),
                        pltpu.SemaphoreType.DMA((2,))],
        compiler_params=_cparams("arbitrary", "arbitrary"),
        name="moe_combine",
    )(dest2, dest2, wgt2, ybuf, x, modtab, gain_row)


def _deinterleave(width):
    idx = np.arange(width).reshape(-1, HEAD_DIM // 2, 2)
    return np.concatenate([idx[:, :, 0], idx[:, :, 1]], axis=1).reshape(-1)


def _rope_tables(ang, ctx_len):
    cos = jnp.cos(ang)
    sin = jnp.sin(ang)
    cos64 = jnp.concatenate([cos, cos], axis=-1)
    sin64 = jnp.concatenate([-sin, sin], axis=-1)
    reps = LANES // HEAD_DIM
    cos_t = jnp.concatenate([jnp.ones((ctx_len, HEAD_DIM), F32), cos64], axis=0)
    sin_t = jnp.concatenate([jnp.zeros((ctx_len, HEAD_DIM), F32), sin64], axis=0)
    return jnp.tile(cos_t, (1, reps)), jnp.tile(sin_t, (1, reps))


def _axial_angles(n_tok):
    rows = n_tok // GRID_W
    row = jnp.repeat(jnp.arange(rows), GRID_W).astype(F32)
    col = jnp.tile(jnp.arange(GRID_W), rows).astype(F32)
    half = HEAD_DIM // 2
    inv = ROPE_THETA ** (-jnp.arange(0, half, 2, dtype=F32) / half)
    return jnp.concatenate([row[:, None] * inv, col[:, None] * inv], axis=-1)


def _line_angles(n_tok):
    pos = jnp.arange(n_tok, dtype=F32)
    inv = ROPE_THETA ** (-jnp.arange(0, RET_QK_DIM, 2, dtype=F32) / RET_QK_DIM)
    return pos[:, None] * inv


def _layer(x, modtab, p, layer_idx, ctx_len, last):
    b, t, d = x.shape
    nct = ctx_len // ROW_TILE
    tile0 = nct if last else 0
    bwid = d // 2
    sizes = (bwid, GQA_KV_HEADS * HEAD_DIM, GQA_KV_HEADS * HEAD_DIM,
             bwid, bwid, bwid,
             RET_HEADS * RET_QK_DIM, RET_HEADS * RET_QK_DIM, bwid, bwid,
             3 * bwid, bwid, 2 * DN_HEADS, 2 * DN_HEADS,
             N_BRANCHES * d)
    cuts = np.concatenate([[0], np.cumsum(sizes)])
    w_in = p['w_in']
    col = lambda a, e: w_in[:, cuts[a]:cuts[e]]
    gains = p['norm_gain']
    ones_row = jnp.ones((1, LANES), F32)

    h = _modulate(x, modtab, gains[0:1], nct)

    seq = t - ctx_len
    cos_ax, sin_ax = _rope_tables(_axial_angles(seq), ctx_len)
    cos_ln, sin_ln = _rope_tables(_line_angles(seq), ctx_len)

    perm_q = _deinterleave(sizes[0])
    perm_k = _deinterleave(sizes[1])
    wa = jnp.concatenate([col(0, 1)[:, perm_q], col(1, 2)[:, perm_k], col(2, 3)], axis=1).astype(BF16)
    perm64 = _deinterleave(HEAD_DIM)
    qk_gain = p['gqa_qk_gain'].astype(F32)
    qscale = HEAD_DIM ** -0.5
    gain_a = jnp.concatenate([jnp.tile(qk_gain[0][perm64] * qscale, sizes[0] // HEAD_DIM),
                              jnp.tile(qk_gain[1][perm64], sizes[1] // HEAD_DIM)]).reshape(1, -1)
    qa, ka, va = _inproj(h, wa, cos_ax, sin_ax, gain_a,
                         [(sizes[0], BF16), (sizes[1], BF16), (sizes[2], BF16)], sizes[0] + sizes[1], True)
    oa = _gqa_attention(qa, ka, va, nct, ctx_len, tile0)

    perm_b = _deinterleave(bwid)
    wb_in = jnp.concatenate([col(3, 4)[:, perm_b] * qscale, col(4, 5)[:, perm_b], col(5, 6)], axis=1).astype(BF16)
    qb, kb, vb = _inproj(h, wb_in, cos_ax, sin_ax, ones_row,
                         [(bwid, BF16)] * 3, 2 * bwid, False)
    lam_init = 0.8 - 0.6 * math.exp(-0.3 * layer_idx)
    ob = _diff_attention(qb, kb, vb, p['diff_lambda'].astype(F32), p['diff_norm'].astype(F32).reshape(1, -1),
                         nct, ctx_len, tile0, lam_init)

    perm_c = _deinterleave(sizes[6])
    wc = jnp.concatenate([col(6, 7)[:, perm_c], col(7, 8)[:, perm_c] * (RET_QK_DIM ** -0.5), col(8, 10)],
                         axis=1).astype(BF16)
    qc, kc, vc, gc = _inproj(h, wc, cos_ln, sin_ln, ones_row,
                             [(sizes[6], BF16), (sizes[7], BF16), (bwid, BF16), (bwid, BF16)],
                             sizes[6] + sizes[7], False)
    log_gamma = jax.nn.log_sigmoid(p['ret_decay_logit'].astype(F32))
    oc_f, oc_b = _retention(qc, kc, vc, log_gamma, ctx_len)

    wd_in = col(10, 12).astype(BF16)
    w_ba = col(12, 14).astype(BF16)
    w_ba = jnp.pad(w_ba, ((0, 0), (0, LANES - w_ba.shape[1])))
    wd_all = jnp.concatenate([wd_in, w_ba], axis=1)
    qkv_d, z_d, ba_d = _inproj(h, wd_all, cos_ln, sin_ln, ones_row,
                               [(3 * bwid, F32), (bwid, BF16), (LANES, F32)], 0, False)
    nb = 2 * DN_HEADS
    rate_row = jnp.exp(p['dn_a_log'].astype(F32)).reshape(1, nb)
    bias_row = p['dn_dt_bias'].astype(F32).reshape(1, nb)
    qd, kd, vd, g_d, beta_d = _dn_prep(qkv_d, ba_d[:, :, :2 * nb], p['dn_conv_w'].astype(F32), rate_row, bias_row, nct)
    od_f, od_b = _dn_scan(*_dn_local(qd, kd, vd, g_d, beta_d), ctx_len)

    wg = col(14, 15).astype(BF16)
    x_new, h2, top_idx, top_w = _merge(h, (oa, ob, oc_f, oc_b, gc, od_f, od_b, z_d),
                                       p['dn_norm'].astype(F32).reshape(1, -1), wg, p['w_branch'].astype(BF16),
                                       p['w_out'].astype(BF16), x, modtab, gains,
                                       p['w_router'].astype(F32), p['b_router'].astype(F32), nct, tile0)

    ne = p['w_router'].shape[1]
    n_routed = b * (t - tile0 * ROW_TILE)
    rank, counts = _moe_rank(top_idx, ne, tile0)
    counts = counts.reshape(ne)
    padded = (counts + MOE_TILE - 1) // MOE_TILE * MOE_TILE
    pad_ends = jnp.cumsum(padded)
    pad_starts = pad_ends - padded
    experts = jnp.arange(ne, dtype=jnp.int32)
    dest = jnp.sum(jnp.where(top_idx[..., None] == experts, pad_starts.astype(jnp.int32), 0), axis=-1) + rank
    n_tiles = (n_routed * TOP_K) // MOE_TILE + ne
    tile_start = jnp.arange(n_tiles, dtype=jnp.int32) * MOE_TILE
    tile_expert = jnp.minimum(jnp.sum(pad_ends[None, :] <= tile_start[:, None], axis=1), ne - 1).astype(jnp.int32)
    n_used = (pad_ends[-1] // MOE_TILE).astype(jnp.int32).reshape(1)
    buf = _moe_dispatch(h2, dest, n_tiles * MOE_TILE, tile0)
    ybuf = _moe_ffn(buf, tile_expert, n_used, p['w_gate_up'].astype(BF16), p['b_gate_up'].astype(F32),
                    p['w_down'].astype(BF16), p['b_down'].astype(F32))
    return _moe_combine(ybuf, dest.reshape(b * t, TOP_K), top_w.reshape(b * t, TOP_K), x_new, modtab, gains[3:4],
                        nct, tile0)


def kernel(x, c, ctx, c_ctx, w_mod, b_mod, norm_gain, w_in, gqa_qk_gain, diff_lambda, diff_norm, ret_decay_logit,
           dn_conv_w, dn_a_log, dn_dt_bias, dn_norm, w_branch, w_out, w_router, b_router, w_gate_up, b_gate_up,
           w_down, b_down):
    b, seq, d = x.shape
    ctx_len = ctx.shape[1]
    depth = w_mod.shape[0]
    assert ctx_len % ROW_TILE == 0 and seq % ROW_TILE == 0 and seq % GRID_W == 0 and d % LANES == 0
    xa = jnp.concatenate([ctx, x], axis=1).astype(F32)
    c_all = jnp.concatenate([c, c_ctx[None, :]], axis=0).astype(F32)
    for l in range(depth):
        p = {
            'norm_gain': norm_gain[l].astype(F32), 'w_in': w_in[l], 'gqa_qk_gain': gqa_qk_gain[l],
            'diff_lambda': diff_lambda[l], 'diff_norm': diff_norm[l], 'ret_decay_logit': ret_decay_logit[l],
            'dn_conv_w': dn_conv_w[l], 'dn_a_log': dn_a_log[l], 'dn_dt_bias': dn_dt_bias[l], 'dn_norm': dn_norm[l],
            'w_branch': w_branch[l], 'w_out': w_out[l], 'w_router': w_router[l], 'b_router': b_router[l],
            'w_gate_up': w_gate_up[l], 'b_gate_up': b_gate_up[l], 'w_down': w_down[l], 'b_down': b_down[l],
        }
        mod = _mod_table(c_all, w_mod[l].astype(F32), b_mod[l].astype(F32))
        mod = mod.reshape(b + 1, N_MOD, d)
        modtab = jnp.stack([jnp.broadcast_to(mod[b], (b, N_MOD, d)), mod[:b]], axis=1)
        xa = _layer(xa, modtab, p, l, ctx_len, l == depth - 1)
    return xa[:, ctx_len:, :]
```
